```python
import jax, jax.numpy as jnp
from jax import lax
import numpy as np

D_MODEL = 1024
BATCH = 8
SEQ = 2048
DEPTH = 1
DEC_BATCH = 128
DEC_SEQ = 1
PAST_LEN = 16384
PAGE_SIZE = 128

D_MIX = D_MODEL
D_A = D_MIX // 2
HEAD_A = 64
H_A = D_A // HEAD_A
LORA_W = 64
LORA_A = 64
LORA_G = 128
P_A = 3 * D_A + LORA_W + LORA_A + LORA_G
A_SPLITS = [D_A, 2 * D_A, 3 * D_A, 3 * D_A + LORA_W, 3 * D_A + LORA_W + LORA_A]
D_C = D_MIX - D_A
G_C = 8
HEAD_C = D_C // G_C
CHUNK = 128
P_C = 2 * D_C
P_TOTAL = P_A + P_C
N_KEYS = 128
N_EXPERTS = N_KEYS * N_KEYS
PEER_HEADS = 8
PEER_TOPK = 16
D_KEY = 256
D_HALF = D_KEY // 2
PEER_BLOCK = 128
D_PLE = 256
RMS_EPS = 1e-6
LN_EPS = 1e-5
GN_EPS = 64e-5

kernel_name = 'rwkv7_chunkgmlp_peer_hybrid_step'


def rmsnorm(x, g):
    xf = x.astype(jnp.float32)
    y = xf * lax.rsqrt(jnp.mean(xf * xf, axis=-1, keepdims=True) + RMS_EPS) * g
    return y.astype(x.dtype)


def rwkv_group(pa, shift_prev, wkv_prev, mu, w0, w2, a0, a2, g2, k_k, k_a, r_k, ln_g, ln_b):
    B, L, _ = pa.shape
    f32 = jnp.float32
    prev_rows = jnp.concatenate([shift_prev[:, None, :].astype(pa.dtype), pa[:, :-1]], axis=1)
    xm = pa + (prev_rows - pa) * mu
    r, k, v, wl, al, gl = jnp.split(xm, A_SPLITS, axis=-1)
    w = -jax.nn.softplus(-(w0 + jnp.tanh(wl) @ w2).astype(f32)) - 0.5
    decay = jnp.exp(-jnp.exp(w))
    a = jax.nn.sigmoid((a0 + al @ a2).astype(f32))
    g = jax.nn.sigmoid(gl) @ g2
    heads = lambda t: t.astype(f32).reshape(B, L, H_A, HEAD_A)
    kk = heads(k * k_k)
    kk = kk / jnp.maximum(jnp.sqrt(jnp.sum(kk * kk, axis=-1, keepdims=True)), 1e-12)
    k_h = heads(k.astype(f32) * (1.0 + (a - 1.0) * k_a))
    r_h, v_h, a_h, w_h = heads(r), heads(v), heads(a), heads(decay)
    to_time = lambda t: jnp.swapaxes(t, 0, 1)

    def step(S, inp):
        r_t, k_t, v_t, w_t, kk_t, a_t = inp
        sa = jnp.einsum('bhvk,bhk->bhv', S, -kk_t)
        S = (S * w_t[:, :, None, :] + sa[..., None] * (kk_t * a_t)[:, :, None, :]
             + v_t[..., None] * k_t[:, :, None, :])
        return S, jnp.einsum('bhvk,bhk->bhv', S, r_t)

    xs = tuple(to_time(t) for t in (r_h, k_h, v_h, w_h, kk, a_h))
    S_fin, ys = lax.scan(step, wkv_prev.astype(f32), xs)
    y = jnp.swapaxes(ys, 0, 1)
    mean = jnp.mean(y, axis=-1, keepdims=True)
    var = jnp.mean(jnp.square(y - mean), axis=-1, keepdims=True)
    yn = ((y - mean) * lax.rsqrt(var + GN_EPS)).reshape(B, L, D_A) * ln_g + ln_b
    bonus = (jnp.sum(r_h * k_h * r_k, axis=-1, keepdims=True) * v_h).reshape(B, L, D_A)
    out = ((yn + bonus) * g).astype(pa.dtype)
    return out, S_fin.astype(wkv_prev.dtype), pa[:, -1]


def chunk_mlp(pc, ln_g, ln_b, ws, bs):
    B, L, _ = pc.shape
    u, v = jnp.split(jax.nn.gelu(pc, approximate=False), 2, axis=-1)
    vg = v.astype(jnp.float32).reshape(B, L, G_C, HEAD_C)
    mean = jnp.mean(vg, axis=-1, keepdims=True)
    var = jnp.mean(jnp.square(vg - mean), axis=-1, keepdims=True)
    vn = (vg - mean) * lax.rsqrt(var + LN_EPS) * ln_g.reshape(G_C, HEAD_C) + ln_b.reshape(G_C, HEAD_C)
    n_chunks = -(-L // CHUNK)
    pad = n_chunks * CHUNK - L
    vp = jnp.pad(vn, ((0, 0), (0, pad), (0, 0), (0, 0))).reshape(B, n_chunks, CHUNK, G_C, HEAD_C)
    causal = jnp.tril(jnp.ones((CHUNK, CHUNK), dtype=bool))
    wsm = jnp.where(causal[None], ws, jnp.zeros_like(ws))
    s = jnp.einsum('gij,bcjgd->bcigd', wsm, vp) + jnp.swapaxes(bs, 0, 1)[None, None, :, :, None]
    s = s.reshape(B, n_chunks * CHUNK, D_C)[:, :L]
    out = (u * s).astype(pc.dtype)
    return out, vn.reshape(B, L, D_C).astype(pc.dtype)


def peer_ffn(h, wq, subkeys, U, V):
    B, L, D = h.shape
    T = B * L
    xt = h.reshape(T, D)
    q = (xt @ wq).reshape(T, PEER_HEADS, 2, D_HALF)
    sc = jnp.einsum('thcd,cnd->thcn', q, subkeys).astype(jnp.float32)
    sv, si = lax.top_k(sc, PEER_TOPK)
    cand = sv[:, :, 0, :, None] + sv[:, :, 1, None, :]
    cand_idx = si[:, :, 0, :, None] * N_KEYS + si[:, :, 1, None, :]
    cv, ci = lax.top_k(cand.reshape(T, PEER_HEADS, PEER_TOPK * PEER_TOPK), PEER_TOPK)
    eidx = jnp.take_along_axis(cand_idx.reshape(T, PEER_HEADS, PEER_TOPK * PEER_TOPK), ci, axis=-1)
    gate = jax.nn.softmax(cv, axis=-1)
    HK = PEER_HEADS * PEER_TOPK
    nb = -(-T // PEER_BLOCK)
    pad = nb * PEER_BLOCK - T
    xb = jnp.pad(xt, ((0, pad), (0, 0))).reshape(nb, PEER_BLOCK, D)
    eb = jnp.pad(eidx.reshape(T, HK), ((0, pad), (0, 0))).reshape(nb, PEER_BLOCK, HK)
    gb = jnp.pad(gate.reshape(T, HK), ((0, pad), (0, 0))).reshape(nb, PEER_BLOCK, HK)

    def block(args):
        x_b, e_b, g_b = args
        u_rows = jnp.take(U, e_b, axis=0)
        act = jax.nn.gelu(jnp.einsum('tkd,td->tk', u_rows, x_b).astype(jnp.float32), approximate=False) * g_b
        v_rows = jnp.take(V, e_b, axis=0)
        return jnp.einsum('tk,tkd->td', act.astype(v_rows.dtype), v_rows)

    out = lax.map(block, (xb, eb, gb)).reshape(nb * PEER_BLOCK, D)[:T]
    return out.reshape(B, L, D).astype(h.dtype)


def decoder_layer(x, p, wkv_prev, shift_prev, lp):
    (norm_mix_g, w_in, shift_mu, rwkv_w0, rwkv_w2, rwkv_a0, rwkv_a2, rwkv_g2,
     rwkv_k_k, rwkv_k_a, rwkv_r_k, rwkv_ln_g, rwkv_ln_b, cmlp_ln_g, cmlp_ln_b,
     cmlp_ws, cmlp_bs, w_out, norm_ffn_g, peer_wq, peer_subkeys, peer_u, peer_v,
     norm_ple_g, ple_w, ple_gate_w) = lp
    proj = rmsnorm(x, norm_mix_g) @ w_in
    ya, wkv_new, shift_new = rwkv_group(proj[..., :P_A], shift_prev, wkv_prev, shift_mu,
                                        rwkv_w0, rwkv_w2, rwkv_a0, rwkv_a2, rwkv_g2,
                                        rwkv_k_k, rwkv_k_a, rwkv_r_k, rwkv_ln_g, rwkv_ln_b)
    yc, v_rows = chunk_mlp(proj[..., P_A:], cmlp_ln_g, cmlp_ln_b, cmlp_ws, cmlp_bs)
    x = x + jnp.concatenate([ya, yc], axis=-1) @ w_out
    x = x + peer_ffn(rmsnorm(x, norm_ffn_g), peer_wq, peer_subkeys, peer_u, peer_v)
    gate = jax.nn.sigmoid((rmsnorm(x, norm_ple_g) @ ple_gate_w).astype(jnp.float32))
    x = x + ((p @ ple_w) * gate).astype(x.dtype)
    return x, wkv_new, shift_new, v_rows


def setup_inputs(seed: int = 0) -> dict:
    key = jax.random.key(seed)
    ks = iter(jax.random.split(key, 40))
    nrm = lambda shape, scale: jax.random.normal(next(ks), shape, jnp.float32) * scale
    uni = lambda shape, lo, hi: jax.random.uniform(next(ks), shape, jnp.float32, lo, hi)
    gain = lambda shape: 1.0 + nrm(shape, 0.02)
    return {
        'x_prompt': nrm((BATCH, SEQ, D_MODEL), 1.0),
        'x_sample': nrm((DEC_BATCH, DEC_SEQ, D_MODEL), 1.0),
        'state_wkv': nrm((DEPTH, DEC_BATCH, H_A, HEAD_A, HEAD_A), 0.3),
        'state_shift': nrm((DEPTH, DEC_BATCH, P_A), 1.0),
        'p_prompt': nrm((DEPTH, BATCH, SEQ, D_PLE), 1.0),
        'p_sample': nrm((DEPTH, DEC_BATCH, DEC_SEQ, D_PLE), 1.0),
        'norm_mix_g': gain((DEPTH, D_MODEL)),
        'w_in': nrm((DEPTH, D_MODEL, P_TOTAL), D_MODEL ** -0.5),
        'shift_mu': uni((DEPTH, P_A), 0.0, 1.0),
        'rwkv_w0': uni((DEPTH, D_A), -4.0, 1.0),
        'rwkv_w2': nrm((DEPTH, LORA_W, D_A), 0.5 * LORA_W ** -0.5),
        'rwkv_a0': nrm((DEPTH, D_A), 0.1),
        'rwkv_a2': nrm((DEPTH, LORA_A, D_A), 0.5 * LORA_A ** -0.5),
        'rwkv_g2': nrm((DEPTH, LORA_G, D_A), LORA_G ** -0.5),
        'rwkv_k_k': 0.85 + nrm((DEPTH, D_A), 0.05),
        'rwkv_k_a': 1.0 + nrm((DEPTH, D_A), 0.05),
        'rwkv_r_k': nrm((DEPTH, H_A, HEAD_A), 0.1),
        'rwkv_ln_g': gain((DEPTH, D_A)),
        'rwkv_ln_b': nrm((DEPTH, D_A), 0.02),
        'cmlp_ln_g': gain((DEPTH, D_C)),
        'cmlp_ln_b': nrm((DEPTH, D_C), 0.02),
        'cmlp_ws': nrm((DEPTH, G_C, CHUNK, CHUNK), CHUNK ** -0.5),
        'cmlp_bs': 1.0 + nrm((DEPTH, G_C, CHUNK), 0.05),
        'w_out': nrm((DEPTH, D_MIX, D_MODEL), D_MIX ** -0.5),
        'norm_ffn_g': gain((DEPTH, D_MODEL)),
        'peer_wq': nrm((DEPTH, D_MODEL, PEER_HEADS * D_KEY), D_MODEL ** -0.5),
        'peer_subkeys': nrm((DEPTH, 2, N_KEYS, D_HALF), D_HALF ** -0.5),
        'peer_u': nrm((DEPTH, N_EXPERTS, D_MODEL), D_MODEL ** -0.5),
        'peer_v': nrm((DEPTH, N_EXPERTS, D_MODEL), 0.3),
        'norm_ple_g': gain((DEPTH, D_MODEL)),
        'ple_w': nrm((DEPTH, D_PLE, D_MODEL), D_PLE ** -0.5),
        'ple_gate_w': nrm((DEPTH, D_MODEL, D_MODEL), D_MODEL ** -0.5),
        'norm_final_g': gain((D_MODEL,)),
    }


def reference(x_prompt, x_sample, state_wkv, state_shift, p_prompt, p_sample,
              norm_mix_g, w_in, shift_mu, rwkv_w0, rwkv_w2, rwkv_a0, rwkv_a2, rwkv_g2,
              rwkv_k_k, rwkv_k_a, rwkv_r_k, rwkv_ln_g, rwkv_ln_b, cmlp_ln_g, cmlp_ln_b,
              cmlp_ws, cmlp_bs, w_out, norm_ffn_g, peer_wq, peer_subkeys, peer_u, peer_v,
              norm_ple_g, ple_w, ple_gate_w, norm_final_g):
    layer_weights = (norm_mix_g, w_in, shift_mu, rwkv_w0, rwkv_w2, rwkv_a0, rwkv_a2, rwkv_g2,
                     rwkv_k_k, rwkv_k_a, rwkv_r_k, rwkv_ln_g, rwkv_ln_b, cmlp_ln_g, cmlp_ln_b,
                     cmlp_ws, cmlp_bs, w_out, norm_ffn_g, peer_wq, peer_subkeys, peer_u, peer_v,
                     norm_ple_g, ple_w, ple_gate_w)
    B = x_prompt.shape[0]
    hp, hs = x_prompt, x_sample
    wkv_p, shift_p, wkv_s, shift_s, cv_s = [], [], [], [], []
    for i in range(DEPTH):
        lp = tuple(w[i] for w in layer_weights)
        wkv0 = jnp.zeros((B, H_A, HEAD_A, HEAD_A), x_prompt.dtype)
        shift0 = jnp.zeros((B, P_A), x_prompt.dtype)
        hp, wkv_pi, shift_pi, _ = decoder_layer(hp, p_prompt[i], wkv0, shift0, lp)
        hs, wkv_si, shift_si, v_si = decoder_layer(hs, p_sample[i], state_wkv[i], state_shift[i], lp)
        wkv_p.append(wkv_pi)
        shift_p.append(shift_pi)
        wkv_s.append(wkv_si)
        shift_s.append(shift_si)
        cv_s.append(v_si)
    y_prompt = rmsnorm(hp, norm_final_g)
    y_sample = rmsnorm(hs, norm_final_g)
    return (y_prompt, y_sample, jnp.stack(wkv_p), jnp.stack(shift_p),
            jnp.stack(wkv_s), jnp.stack(shift_s), jnp.stack(cv_s))
```

```python
import functools
import math

import jax
import jax.numpy as jnp
from jax import lax
from jax.experimental import pallas as pl
from jax.experimental.pallas import tpu as pltpu

F32 = jnp.float32
BF16 = jnp.bfloat16
HIGHEST = lax.Precision.HIGHEST

LANES = 128
HEAD = 64
CHUNK = 128
N_KEYS = 128
PEER_TOPK = 16
RMS_EPS = 1e-6
LN_EPS = 1e-5
GN_EPS = 64e-5
SQRT_HALF = math.sqrt(0.5)
NEG_INF = float("-inf")
VMEM_LIMIT = 56 * 1024 * 1024


def _cparams(*sem):
    return pltpu.CompilerParams(dimension_semantics=sem, vmem_limit_bytes=VMEM_LIMIT)


def _rmsnorm(x, g):
    return x * lax.rsqrt(jnp.mean(x * x, axis=-1, keepdims=True) + RMS_EPS) * g


def _gelu(x):
    return 0.5 * x * (1.0 + lax.erf(x * SQRT_HALF))


def _group_sum(x, bd):
    return jnp.dot(x, bd, precision=HIGHEST, preferred_element_type=F32)


def _row_tile(n, want):
    t = min(n, want)
    assert n % t == 0, (n, t)
    return t


def _full(shape):
    return pl.BlockSpec(shape, lambda *_: (0,) * len(shape))


def _proj_kernel(x_ref, g_ref, w_ref, pa_ref, pc_ref, *, p_a):
    h = _rmsnorm(x_ref[...], g_ref[...])
    p = jnp.dot(h.astype(BF16), w_ref[...], preferred_element_type=F32)
    pa_ref[...] = p[:, :p_a]
    pc_ref[...] = p[:, p_a:]


def _proj(x, g, w_bf, p_a):
    t, d = x.shape
    p_tot = w_bf.shape[1]
    tm = _row_tile(t, 512)
    return pl.pallas_call(
        functools.partial(_proj_kernel, p_a=p_a),
        grid=(t // tm,),
        in_specs=[pl.BlockSpec((tm, d), lambda i: (i, 0)), _full((1, d)), _full((d, p_tot))],
        out_specs=[pl.BlockSpec((tm, p_a), lambda i: (i, 0)),
                   pl.BlockSpec((tm, p_tot - p_a), lambda i: (i, 0))],
        out_shape=[jax.ShapeDtypeStruct((t, p_a), F32), jax.ShapeDtypeStruct((t, p_tot - p_a), F32)],
        compiler_params=_cparams("parallel"),
        name="proj",
    )(x, g, w_bf)


def _prep_kernel(pa_ref, prev_ref, mu_ref, w0_ref, a0_ref, wa2_ref, g2_ref, kk_ref, ka_ref, rk_ref, bd_ref,
                 r_o, k_o, v_o, w_o, a_o, b_o, g_o, bonus_o, *, d_a):
    pa = pa_ref[...]
    xm = pa + (prev_ref[...] - pa) * mu_ref[...]
    r = xm[:, 0:d_a]
    k = xm[:, d_a:2 * d_a]
    v = xm[:, 2 * d_a:3 * d_a]
    wal = xm[:, 3 * d_a:3 * d_a + LANES]
    gl = xm[:, 3 * d_a + LANES:]
    lane = lax.broadcasted_iota(jnp.int32, wal.shape, 1)
    wal = jnp.where(lane < HEAD, jnp.tanh(wal), wal)
    lo = jnp.dot(wal.astype(BF16), wa2_ref[...], preferred_element_type=F32)
    z = -(w0_ref[...] + lo[:, :d_a])
    softplus = jnp.maximum(z, 0.0) + jnp.log1p(jnp.exp(-jnp.abs(z)))
    decay = jnp.exp(-jnp.exp(-softplus - 0.5))
    a = jax.nn.sigmoid(a0_ref[...] + lo[:, d_a:])
    g = jnp.dot(jax.nn.sigmoid(gl).astype(BF16), g2_ref[...], preferred_element_type=F32)
    bd = bd_ref[...]
    kk = k * kk_ref[...]
    kk = kk / jnp.maximum(jnp.sqrt(_group_sum(kk * kk, bd)), 1e-12)
    kh = k * (1.0 + (a - 1.0) * ka_ref[...])
    r_o[...] = r
    k_o[...] = kh
    v_o[...] = v
    w_o[...] = decay
    a_o[...] = -kk
    b_o[...] = kk * a
    g_o[...] = g
    bonus_o[...] = _group_sum(r * kh * rk_ref[...], bd) * v


def _prep(pa, prev, mu, w0, a0, wa2_bf, g2_bf, k_k, k_a, r_k, bd):
    t, p_a = pa.shape
    d_a = w0.shape[1]
    tm = _row_tile(t, 256)
    row = lambda n: pl.BlockSpec((tm, n), lambda i: (i, 0))
    return pl.pallas_call(
        functools.partial(_prep_kernel, d_a=d_a),
        grid=(t // tm,),
        in_specs=[row(p_a), row(p_a), _full((1, p_a)), _full((1, d_a)), _full((1, d_a)),
                  _full(wa2_bf.shape), _full(g2_bf.shape), _full((1, d_a)), _full((1, d_a)),
                  _full((1, d_a)), _full(bd.shape)],
        out_specs=[row(d_a)] * 8,
        out_shape=[jax.ShapeDtypeStruct((t, d_a), F32)] * 8,
        compiler_params=_cparams("parallel"),
        name="rwkv_prep",
    )(pa, prev, mu, w0, a0, wa2_bf, g2_bf, k_k, k_a, r_k, bd)


def _scan_kernel(w_ref, a_ref, b_ref, k_ref, r_ref, v_ref, s0_ref, y_ref, sfin_ref, st_ref, *, tl, nv):
    l = pl.program_id(1)

    @pl.when(l == 0)
    def _():
        st_ref[...] = s0_ref[...]

    def step(t, carry):
        for i in range(nv):
            s = st_ref[i]
            sa = jnp.sum(s * a_ref[t], axis=0, keepdims=True)
            s = s * w_ref[t] + sa * b_ref[t] + v_ref[t, pl.ds(i, 1), :] * k_ref[t]
            st_ref[i] = s
            y_ref[t, pl.ds(i, 1), :] = jnp.sum(s * r_ref[t], axis=0, keepdims=True)
        return carry

    lax.fori_loop(0, tl, step, 0)

    @pl.when(l == pl.num_programs(1) - 1)
    def _():
        sfin_ref[...] = st_ref[...]


def _scan(w, a, b, k, r, v, s0):
    l, n, gl = w.shape
    nv = v.shape[1]
    tl = _row_tile(l, 64)
    rows = pl.BlockSpec((tl, n, LANES), lambda g, i: (i, 0, g))
    vals = pl.BlockSpec((tl, nv, LANES), lambda g, i: (i, 0, g))
    state = pl.BlockSpec((nv, n, LANES), lambda g, i: (0, 0, g))
    return pl.pallas_call(
        functools.partial(_scan_kernel, tl=tl, nv=nv),
        grid=(gl // LANES, l // tl),
        in_specs=[rows] * 5 + [vals, state],
        out_specs=[vals, state],
        out_shape=[jax.ShapeDtypeStruct((l, nv, gl), F32), jax.ShapeDtypeStruct((nv, n, gl), F32)],
        scratch_shapes=[pltpu.VMEM((nv, n, LANES), F32)],
        compiler_params=_cparams("parallel", "arbitrary"),
        name="rwkv_scan",
    )(w, a, b, k, r, v, s0)


def _cmlp_norm(pc, lng, lnb, bd, d_c):
    ge = _gelu(pc)
    u = ge[:, :d_c]
    v = ge[:, d_c:]
    mean = _group_sum(v, bd) * (1.0 / HEAD)
    d = v - mean
    var = _group_sum(d * d, bd) * (1.0 / HEAD)
    return u, d * lax.rsqrt(var + LN_EPS) * lng + lnb


def _cmlp_kernel(pc_ref, lng_ref, lnb_ref, ws_ref, bs_ref, bd_ref, yc_ref, *, d_c, n_chunks):
    u, vn = _cmlp_norm(pc_ref[...], lng_ref[...], lnb_ref[...], bd_ref[...], d_c)
    vb = vn.astype(BF16)
    ri = lax.broadcasted_iota(jnp.int32, (CHUNK, CHUNK), 0)
    ci = lax.broadcasted_iota(jnp.int32, (CHUNK, CHUNK), 1)
    lane = lax.broadcasted_iota(jnp.int32, (CHUNK, LANES), 1)
    n_groups = d_c // HEAD
    wsm = [jnp.where(ri >= ci, ws_ref[g], 0.0).astype(BF16) for g in range(n_groups)]
    for c in range(n_chunks):
        rows = slice(c * CHUNK, (c + 1) * CHUNK)
        for p in range(n_groups // 2):
            cols = slice(p * LANES, (p + 1) * LANES)
            vp = vb[rows, cols]
            s0 = jnp.dot(wsm[2 * p], vp, preferred_element_type=F32)
            s1 = jnp.dot(wsm[2 * p + 1], vp, preferred_element_type=F32)
            s = jnp.where(lane < HEAD, s0, s1) + bs_ref[:, cols]
            yc_ref[rows, cols] = u[rows, cols] * s


def _cmlp(pc, lng, lnb, ws, bs_rows, bd):
    t, p_c = pc.shape
    d_c = p_c // 2
    tm = _row_tile(t, 256)
    assert tm % CHUNK == 0
    return pl.pallas_call(
        functools.partial(_cmlp_kernel, d_c=d_c, n_chunks=tm // CHUNK),
        grid=(t // tm,),
        in_specs=[pl.BlockSpec((tm, p_c), lambda i: (i, 0)), _full((1, d_c)), _full((1, d_c)),
                  _full(ws.shape), _full(bs_rows.shape), _full(bd.shape)],
        out_specs=pl.BlockSpec((tm, d_c), lambda i: (i, 0)),
        out_shape=jax.ShapeDtypeStruct((t, d_c), F32),
        compiler_params=_cparams("parallel"),
        name="cmlp",
    )(pc, lng, lnb, ws, bs_rows, bd)


def _cmlp_first_kernel(pc_ref, lng_ref, lnb_ref, w00_ref, b0_ref, bd_ref, yc_ref, vn_ref, *, d_c):
    u, vn = _cmlp_norm(pc_ref[...], lng_ref[...], lnb_ref[...], bd_ref[...], d_c)
    yc_ref[...] = u * (w00_ref[...] * vn + b0_ref[...])
    vn_ref[...] = vn


def _cmlp_first(pc, lng, lnb, w00, b0, bd):
    t, p_c = pc.shape
    d_c = p_c // 2
    return pl.pallas_call(
        functools.partial(_cmlp_first_kernel, d_c=d_c),
        grid=(1,),
        in_specs=[_full(pc.shape), _full((1, d_c)), _full((1, d_c)), _full((1, d_c)), _full((1, d_c)),
                  _full(bd.shape)],
        out_specs=[_full((t, d_c))] * 2,
        out_shape=[jax.ShapeDtypeStruct((t, d_c), F32)] * 2,
        compiler_params=_cparams("arbitrary"),
        name="cmlp_first",
    )(pc, lng, lnb, w00, b0, bd)


def _mix_kernel(x_ref, y_ref, bonus_ref, g_ref, yc_ref, lng_ref, lnb_ref, bd_ref, wo_ref, o_ref, *, d_a):
    bd = bd_ref[...]
    y = y_ref[...]
    mean = _group_sum(y, bd) * (1.0 / HEAD)
    d = y - mean
    var = _group_sum(d * d, bd) * (1.0 / HEAD)
    yn = d * lax.rsqrt(var + GN_EPS) * lng_ref[...] + lnb_ref[...]
    ya = (yn + bonus_ref[...]) * g_ref[...]
    o_ref[...] = (x_ref[...]
                  + jnp.dot(ya.astype(BF16), wo_ref[0:d_a, :], preferred_element_type=F32)
                  + jnp.dot(yc_ref[...].astype(BF16), wo_ref[d_a:, :], preferred_element_type=F32))


def _mix(x, y, bonus, g, yc, lng, lnb, bd, wo_bf):
    t, d = x.shape
    d_a = y.shape[1]
    d_c = yc.shape[1]
    tm = _row_tile(t, 256)
    row = lambda n: pl.BlockSpec((tm, n), lambda i: (i, 0))
    return pl.pallas_call(
        functools.partial(_mix_kernel, d_a=d_a),
        grid=(t // tm,),
        in_specs=[row(d), row(d_a), row(d_a), row(d_a), row(d_c), _full((1, d_a)), _full((1, d_a)),
                  _full(bd.shape), _full(wo_bf.shape)],
        out_specs=row(d),
        out_shape=jax.ShapeDtypeStruct((t, d), F32),
        compiler_params=_cparams("parallel"),
        name="mix_out",
    )(x, y, bonus, g, yc, lng, lnb, bd, wo_bf)


def _peerq_kernel(x_ref, g_ref, wq_ref, sk_ref, xn_ref, st_ref, *, n_hc):
    hb = _rmsnorm(x_ref[...], g_ref[...]).astype(BF16)
    xn_ref[...] = hb
    qb = jnp.dot(hb, wq_ref[...], preferred_element_type=F32).astype(BF16)
    for hc in range(n_hc):
        st_ref[hc] = lax.dot_general(sk_ref[hc % 2], qb[:, hc * LANES:(hc + 1) * LANES],
                                     (((1,), (1,)), ((), ())), preferred_element_type=F32)


def _peerq(x, g, wq_bf, sk_bf):
    t, d = x.shape
    n_hc = wq_bf.shape[1] // LANES
    tm = _row_tile(t, 512)
    return pl.pallas_call(
        functools.partial(_peerq_kernel, n_hc=n_hc),
        grid=(t // tm,),
        in_specs=[pl.BlockSpec((tm, d), lambda i: (i, 0)), _full((1, d)), _full(wq_bf.shape),
                  _full(sk_bf.shape)],
        out_specs=[pl.BlockSpec((tm, d), lambda i: (i, 0)),
                   pl.BlockSpec((n_hc, N_KEYS, tm), lambda i: (0, 0, i))],
        out_shape=[jax.ShapeDtypeStruct((t, d), BF16), jax.ShapeDtypeStruct((n_hc, N_KEYS, t), F32)],
        compiler_params=_cparams("parallel"),
        name="peer_query",
    )(x, g, wq_bf, sk_bf)


def _top_values(s, n):
    n_pad = -(-n // 8) * 8
    row = lax.broadcasted_iota(jnp.int32, (n_pad, s.shape[1]), 0)
    out = jnp.full((n_pad, s.shape[1]), NEG_INF, F32)
    cur = s
    for j in range(n):
        m = jnp.max(cur, axis=0, keepdims=True)
        out = jnp.where(row == j, m, out)
        cur = jnp.where(cur == m, NEG_INF, cur)
    return out


def _peer_topk_kernel(st_ref, p1_ref, p2_ref, th_ref, *, n_heads):
    k = PEER_TOPK

    def head(h, carry):
        s1 = st_ref[2 * h]
        s2 = st_ref[2 * h + 1]
        v1 = _top_values(s1, k + 1)
        v2 = _top_values(s2, k + 1)
        row = lax.broadcasted_iota(jnp.int32, v2.shape, 0)
        cands = []
        for i in range(k + 1):
            n_j = (k + 1) // (i + 1)
            rows = v2 if n_j > 8 else v2[0:8]
            cands.append(jnp.where(row[0:rows.shape[0]] < n_j, v1[i:i + 1] + rows, NEG_INF))
        cand = jnp.concatenate(cands, axis=0)
        top = _top_values(cand, k + 1)
        m = top[0:1]
        c_k = top[k - 1:k]
        z = jnp.sum(jnp.where(cand >= c_k, jnp.exp(cand - m), 0.0), axis=0, keepdims=True)
        inv_z = 1.0 / z
        th_ref[pl.ds(h, 1), :] = jnp.exp(0.5 * (c_k + top[k:k + 1]) - m) * inv_z
        p1_ref[h] = jnp.exp(s1 - v1[0:1]) * inv_z
        p2_ref[h] = jnp.exp(s2 - v2[0:1])
        return carry

    lax.fori_loop(0, n_heads, head, 0)


def _peer_topk(st):
    n_hc, n, t = st.shape
    n_heads = n_hc // 2
    tk = LANES
    fac = pl.BlockSpec((n_heads, n, tk), lambda i: (0, 0, i))
    return pl.pallas_call(
        functools.partial(_peer_topk_kernel, n_heads=n_heads),
        grid=(t // tk,),
        in_specs=[pl.BlockSpec((n_hc, n, tk), lambda i: (0, 0, i))],
        out_specs=[fac, fac, pl.BlockSpec((n_heads, tk), lambda i: (0, i))],
        out_shape=[jax.ShapeDtypeStruct((n_heads, n, t), F32)] * 2 + [jax.ShapeDtypeStruct((n_heads, t), F32)],
        compiler_params=_cparams("parallel"),
        name="peer_topk",
    )(st)


ROW_BLK = 32
KEYS_PER_TILE = 8


def _peer_dense_kernel(xn_ref, x_ref, u_ref, vt_ref, p1_ref, p2_ref, th_ref, o_ref, ht_ref, at_ref, acc_ref,
                       *, n_heads, tm):
    j = pl.program_id(1)

    @pl.when(j == 0)
    def _():
        acc_ref[...] = jnp.zeros_like(acc_ref)

    ht_ref[...] = lax.dot_general(u_ref[...], xn_ref[...], (((1,), (1,)), ((), ())),
                                  preferred_element_type=F32)

    for s in range(KEYS_PER_TILE):
        def gate_block(rb, carry, s=s):
            r2 = pl.multiple_of(rb * ROW_BLK, ROW_BLK)
            r0 = pl.multiple_of(s * N_KEYS + rb * ROW_BLK, ROW_BLK)
            for lc in range(tm // LANES):
                cols = pl.ds(lc * LANES, LANES)
                gsum = jnp.zeros((ROW_BLK, LANES), F32)
                for h in range(n_heads):
                    w = p1_ref[h, s:s + 1, cols] * p2_ref[h, pl.ds(r2, ROW_BLK), cols]
                    gsum = gsum + jnp.where(w >= th_ref[h:h + 1, cols], w, 0.0)
                at_ref[pl.ds(r0, ROW_BLK), cols] = (_gelu(ht_ref[pl.ds(r0, ROW_BLK), cols]) * gsum).astype(BF16)
            return carry

        lax.fori_loop(0, N_KEYS // ROW_BLK, gate_block, 0)

    acc_ref[...] += jnp.dot(vt_ref[...], at_ref[...], preferred_element_type=F32)

    @pl.when(j == pl.num_programs(1) - 1)
    def _():
        o_ref[...] = x_ref[...] + acc_ref[...].T


def _peer_dense(xn, x, u_bf, vt_bf, p1, p2, th):
    t, d = x.shape
    n_exp = u_bf.shape[0]
    n_heads = p1.shape[0]
    tm = _row_tile(t, 512)
    te = KEYS_PER_TILE * N_KEYS
    assert n_exp == N_KEYS * N_KEYS
    return pl.pallas_call(
        functools.partial(_peer_dense_kernel, n_heads=n_heads, tm=tm),
        grid=(t // tm, n_exp // te),
        in_specs=[pl.BlockSpec((tm, d), lambda i, j: (i, 0)),
                  pl.BlockSpec((tm, d), lambda i, j: (i, 0)),
                  pl.BlockSpec((te, d), lambda i, j: (j, 0)),
                  pl.BlockSpec((d, te), lambda i, j: (0, j)),
                  pl.BlockSpec((n_heads, KEYS_PER_TILE, tm), lambda i, j: (0, j, i)),
                  pl.BlockSpec((n_heads, N_KEYS, tm), lambda i, j: (0, 0, i)),
                  pl.BlockSpec((n_heads, tm), lambda i, j: (0, i))],
        out_specs=pl.BlockSpec((tm, d), lambda i, j: (i, 0)),
        out_shape=jax.ShapeDtypeStruct((t, d), F32),
        scratch_shapes=[pltpu.VMEM((te, tm), F32), pltpu.VMEM((te, tm), BF16), pltpu.VMEM((d, tm), F32)],
        compiler_params=_cparams("parallel", "arbitrary"),
        name="peer_dense",
    )(xn, x, u_bf, vt_bf, p1, p2, th)


def _ple_kernel(x_ref, p_ref, g_ref, gw_ref, pw_ref, gf_ref, o_ref, *, final):
    x = x_ref[...]
    hn = _rmsnorm(x, g_ref[...])
    gate = jax.nn.sigmoid(jnp.dot(hn.astype(BF16), gw_ref[...], preferred_element_type=F32))
    e = jnp.dot(p_ref[...].astype(BF16), pw_ref[...], preferred_element_type=F32)
    x = x + e * gate
    o_ref[...] = _rmsnorm(x, gf_ref[...]) if final else x


def _ple(x, p, g, gw_bf, pw_bf, gf, final):
    t, d = x.shape
    dp = p.shape[1]
    tm = _row_tile(t, 512)
    return pl.pallas_call(
        functools.partial(_ple_kernel, final=final),
        grid=(t // tm,),
        in_specs=[pl.BlockSpec((tm, d), lambda i: (i, 0)), pl.BlockSpec((tm, dp), lambda i: (i, 0)),
                  _full((1, d)), _full(gw_bf.shape), _full(pw_bf.shape), _full((1, d))],
        out_specs=pl.BlockSpec((tm, d), lambda i: (i, 0)),
        out_shape=jax.ShapeDtypeStruct((t, d), F32),
        compiler_params=_cparams("parallel"),
        name="ple",
    )(x, p, g, gw_bf, pw_bf, gf)


def _rwkv_prompt_scan(parts, batch, seq, n_heads):
    bh = batch * n_heads
    assert 2 * bh == LANES

    def rows(x):
        y = x.reshape(batch, seq, n_heads, HEAD).transpose(1, 3, 0, 2).reshape(seq, HEAD, bh)
        return jnp.concatenate([y, y], axis=-1)

    r, k, v, w, a, b = parts
    vt = v.reshape(batch, seq, n_heads, 2, HEAD // 2).transpose(1, 4, 3, 0, 2).reshape(seq, HEAD // 2, LANES)
    s0 = jnp.zeros((HEAD // 2, HEAD, LANES), F32)
    y, s_fin = _scan(rows(w), rows(a), rows(b), rows(k), rows(r), vt, s0)
    y = y.reshape(seq, HEAD // 2, 2, batch, n_heads).transpose(3, 0, 4, 2, 1).reshape(batch * seq, n_heads * HEAD)
    s_fin = s_fin.reshape(HEAD // 2, HEAD, 2, batch, n_heads).transpose(3, 4, 2, 0, 1)
    return y, s_fin.reshape(batch, n_heads, HEAD, HEAD)


def _rwkv_sample_scan(parts, wkv_prev, batch, n_heads):
    bh = batch * n_heads
    assert bh % LANES == 0
    rows = lambda x: x.reshape(batch, n_heads, HEAD).transpose(2, 0, 1).reshape(1, HEAD, bh)
    r, k, v, w, a, b = parts
    s0 = wkv_prev.transpose(2, 3, 0, 1).reshape(HEAD, HEAD, bh)
    y, s_fin = _scan(rows(w), rows(a), rows(b), rows(k), rows(r), rows(v), s0)
    y = y.reshape(HEAD, batch, n_heads).transpose(1, 2, 0).reshape(batch, n_heads * HEAD)
    return y, s_fin.reshape(HEAD, HEAD, batch, n_heads).transpose(2, 3, 0, 1)


def _layer(x, p, wkv_prev, shift_prev, lw, gf, final):
    batch, seq, d = x.shape
    t = batch * seq
    d_a = lw["w0"].shape[1]
    n_heads = d_a // HEAD
    p_a = lw["mu"].shape[1]
    xt = x.reshape(t, d)

    pa, pc = _proj(xt, lw["norm_mix_g"], lw["w_in"], p_a)
    pa3 = pa.reshape(batch, seq, p_a)
    prev = jnp.concatenate([shift_prev[:, None, :], pa3[:, :-1]], axis=1).reshape(t, p_a)
    r, k, v, w, a, b, g, bonus = _prep(pa, prev, lw["mu"], lw["w0"], lw["a0"], lw["wa2"], lw["g2"],
                                       lw["k_k"], lw["k_a"], lw["r_k"], lw["bd"])
    if wkv_prev is None:
        y, wkv_new = _rwkv_prompt_scan((r, k, v, w, a, b), batch, seq, n_heads)
        yc = _cmlp(pc, lw["cmlp_ln_g"], lw["cmlp_ln_b"], lw["cmlp_ws"], lw["cmlp_bs_rows"], lw["bd"])
        v_rows = None
    else:
        assert seq == 1
        y, wkv_new = _rwkv_sample_scan((r, k, v, w, a, b), wkv_prev, batch, n_heads)
        yc, v_rows = _cmlp_first(pc, lw["cmlp_ln_g"], lw["cmlp_ln_b"], lw["cmlp_w00"], lw["cmlp_b0"], lw["bd"])
        v_rows = v_rows.reshape(batch, seq, -1)
    x1 = _mix(xt, y, bonus, g, yc, lw["rwkv_ln_g"], lw["rwkv_ln_b"], lw["bd"], lw["w_out"])

    xn, st = _peerq(x1, lw["norm_ffn_g"], lw["peer_wq"], lw["peer_subkeys"])
    p1, p2, th = _peer_topk(st)
    x2 = _peer_dense(xn, x1, lw["peer_u"], lw["peer_vt"], p1, p2, th)

    x3 = _ple(x2, p.reshape(t, -1), lw["norm_ple_g"], lw["ple_gate_w"], lw["ple_w"], gf, final)
    return x3.reshape(batch, seq, d), wkv_new, pa3[:, -1], v_rows


def kernel(x_prompt, x_sample, state_wkv, state_shift, p_prompt, p_sample, norm_mix_g, w_in, shift_mu, rwkv_w0, rwkv_w2, rwkv_a0, rwkv_a2, rwkv_g2, rwkv_k_k, rwkv_k_a, rwkv_r_k, rwkv_ln_g, rwkv_ln_b, cmlp_ln_g, cmlp_ln_b, cmlp_ws, cmlp_bs, w_out, norm_ffn_g, peer_wq, peer_subkeys, peer_u, peer_v, norm_ple_g, ple_w, ple_gate_w, norm_final_g):
    depth = state_wkv.shape[0]
    batch = x_prompt.shape[0]
    d_a = rwkv_w0.shape[1]
    d_c = cmlp_ln_g.shape[1]
    p_a = shift_mu.shape[1]
    lora_w = rwkv_w2.shape[1]
    lora_a = rwkv_a2.shape[1]
    assert lora_w == HEAD and lora_a == HEAD and d_a == d_c
    gidx = jnp.arange(d_a) // HEAD
    bd = (gidx[:, None] == gidx[None, :]).astype(F32)
    row = lambda z: z.reshape(1, -1)
    gf = row(norm_final_g)

    hp, hs = x_prompt, x_sample
    outs = [[] for _ in range(5)]
    for i in range(depth):
        wa2 = jnp.zeros((lora_w + lora_a, 2 * d_a), F32)
        wa2 = wa2.at[:lora_w, :d_a].set(rwkv_w2[i]).at[lora_w:, d_a:].set(rwkv_a2[i])
        lw = dict(
            norm_mix_g=row(norm_mix_g[i]), w_in=w_in[i].astype(BF16), mu=row(shift_mu[i]),
            w0=row(rwkv_w0[i]), a0=row(rwkv_a0[i]), wa2=wa2.astype(BF16), g2=rwkv_g2[i].astype(BF16),
            k_k=row(rwkv_k_k[i]), k_a=row(rwkv_k_a[i]), r_k=row(rwkv_r_k[i]), bd=bd,
            rwkv_ln_g=row(rwkv_ln_g[i]), rwkv_ln_b=row(rwkv_ln_b[i]),
            cmlp_ln_g=row(cmlp_ln_g[i]), cmlp_ln_b=row(cmlp_ln_b[i]), cmlp_ws=cmlp_ws[i],
            cmlp_bs_rows=jnp.repeat(cmlp_bs[i].T, HEAD, axis=1),
            cmlp_w00=row(jnp.repeat(cmlp_ws[i][:, 0, 0], HEAD)), cmlp_b0=row(jnp.repeat(cmlp_bs[i][:, 0], HEAD)),
            w_out=w_out[i].astype(BF16), norm_ffn_g=row(norm_ffn_g[i]), peer_wq=peer_wq[i].astype(BF16),
            peer_subkeys=peer_subkeys[i].astype(BF16), peer_u=peer_u[i].astype(BF16),
            peer_vt=peer_v[i].astype(BF16).T, norm_ple_g=row(norm_ple_g[i]),
            ple_w=ple_w[i].astype(BF16), ple_gate_w=ple_gate_w[i].astype(BF16),
        )
        final = i == depth - 1
        shift0 = jnp.zeros((batch, p_a), F32)
        hp, wkv_p, shift_p, _ = _layer(hp, p_prompt[i], None, shift0, lw, gf, final)
        hs, wkv_s, shift_s, v_s = _layer(hs, p_sample[i], state_wkv[i], state_shift[i], lw, gf, final)
        for o, val in zip(outs, (wkv_p, shift_p, wkv_s, shift_s, v_s)):
            o.append(val)
    return (hp, hs) + tuple(jnp.stack(o) for o in outs)
```

```python
import functools
import math

import jax
import jax.numpy as jnp
from jax import lax
from jax.experimental import pallas as pl
from jax.experimental.pallas import tpu as pltpu

F32 = jnp.float32
BF16 = jnp.bfloat16
HIGHEST = lax.Precision.HIGHEST

LANES = 128
HEAD = 64
CHUNK = 128
N_KEYS = 128
PEER_TOPK = 16
RMS_EPS = 1e-6
LN_EPS = 1e-5
GN_EPS = 64e-5
SQRT_HALF = math.sqrt(0.5)
NEG_INF = float("-inf")
VMEM_LIMIT = 56 * 1024 * 1024


def _cparams(*sem):
    return pltpu.CompilerParams(dimension_semantics=sem, vmem_limit_bytes=VMEM_LIMIT)


def _rmsnorm(x, g):
    return x * lax.rsqrt(jnp.mean(x * x, axis=-1, keepdims=True) + RMS_EPS) * g


def _gelu(x):
    return 0.5 * x * (1.0 + lax.erf(x * SQRT_HALF))


def _group_sum(x, bd):
    return jnp.dot(x, bd, precision=HIGHEST, preferred_element_type=F32)


def _row_tile(n, want):
    t = min(n, want)
    assert n % t == 0, (n, t)
    return t


def _full(shape):
    return pl.BlockSpec(shape, lambda *_: (0,) * len(shape))


def _proj_kernel(x_ref, g_ref, w_ref, pa_ref, pc_ref, *, p_a):
    h = _rmsnorm(x_ref[...], g_ref[...])
    p = jnp.dot(h.astype(BF16), w_ref[...], preferred_element_type=F32)
    pa_ref[...] = p[:, :p_a]
    pc_ref[...] = p[:, p_a:]


def _proj(x, g, w_bf, p_a):
    t, d = x.shape
    p_tot = w_bf.shape[1]
    tm = _row_tile(t, 512)
    return pl.pallas_call(
        functools.partial(_proj_kernel, p_a=p_a),
        grid=(t // tm,),
        in_specs=[pl.BlockSpec((tm, d), lambda i: (i, 0)), _full((1, d)), _full((d, p_tot))],
        out_specs=[pl.BlockSpec((tm, p_a), lambda i: (i, 0)),
                   pl.BlockSpec((tm, p_tot - p_a), lambda i: (i, 0))],
        out_shape=[jax.ShapeDtypeStruct((t, p_a), F32), jax.ShapeDtypeStruct((t, p_tot - p_a), F32)],
        compiler_params=_cparams("parallel"),
        name="proj",
    )(x, g, w_bf)


def _prep_kernel(pa_ref, prev_ref, mu_ref, w0_ref, a0_ref, wa2_ref, g2_ref, kk_ref, ka_ref, rk_ref, bd_ref,
                 r_o, k_o, v_o, w_o, a_o, b_o, g_o, bonus_o, *, d_a):
    pa = pa_ref[...]
    xm = pa + (prev_ref[...] - pa) * mu_ref[...]
    r = xm[:, 0:d_a]
    k = xm[:, d_a:2 * d_a]
    v = xm[:, 2 * d_a:3 * d_a]
    wal = xm[:, 3 * d_a:3 * d_a + LANES]
    gl = xm[:, 3 * d_a + LANES:]
    lane = lax.broadcasted_iota(jnp.int32, wal.shape, 1)
    wal = jnp.where(lane < HEAD, jnp.tanh(wal), wal)
    lo = jnp.dot(wal.astype(BF16), wa2_ref[...], preferred_element_type=F32)
    z = -(w0_ref[...] + lo[:, :d_a])
    softplus = jnp.maximum(z, 0.0) + jnp.log1p(jnp.exp(-jnp.abs(z)))
    decay = jnp.exp(-jnp.exp(-softplus - 0.5))
    a = jax.nn.sigmoid(a0_ref[...] + lo[:, d_a:])
    g = jnp.dot(jax.nn.sigmoid(gl).astype(BF16), g2_ref[...], preferred_element_type=F32)
    bd = bd_ref[...]
    kk = k * kk_ref[...]
    kk = kk / jnp.maximum(jnp.sqrt(_group_sum(kk * kk, bd)), 1e-12)
    kh = k * (1.0 + (a - 1.0) * ka_ref[...])
    r_o[...] = r
    k_o[...] = kh
    v_o[...] = v
    w_o[...] = decay
    a_o[...] = -kk
    b_o[...] = kk * a
    g_o[...] = g
    bonus_o[...] = _group_sum(r * kh * rk_ref[...], bd) * v


def _prep(pa, prev, mu, w0, a0, wa2_bf, g2_bf, k_k, k_a, r_k, bd):
    t, p_a = pa.shape
    d_a = w0.shape[1]
    tm = _row_tile(t, 256)
    row = lambda n: pl.BlockSpec((tm, n), lambda i: (i, 0))
    return pl.pallas_call(
        functools.partial(_prep_kernel, d_a=d_a),
        grid=(t // tm,),
        in_specs=[row(p_a), row(p_a), _full((1, p_a)), _full((1, d_a)), _full((1, d_a)),
                  _full(wa2_bf.shape), _full(g2_bf.shape), _full((1, d_a)), _full((1, d_a)),
                  _full((1, d_a)), _full(bd.shape)],
        out_specs=[row(d_a)] * 8,
        out_shape=[jax.ShapeDtypeStruct((t, d_a), F32)] * 8,
        compiler_params=_cparams("parallel"),
        name="rwkv_prep",
    )(pa, prev, mu, w0, a0, wa2_bf, g2_bf, k_k, k_a, r_k, bd)


def _scan_kernel(w_ref, a_ref, b_ref, k_ref, r_ref, v_ref, s0_ref, y_ref, sfin_ref, st_ref, *, tl, nv):
    l = pl.program_id(1)

    @pl.when(l == 0)
    def _():
        st_ref[...] = s0_ref[...]

    def step(t, carry):
        for i in range(nv):
            s = st_ref[i]
            sa = jnp.sum(s * a_ref[t], axis=0, keepdims=True)
            s = s * w_ref[t] + sa * b_ref[t] + v_ref[t, pl.ds(i, 1), :] * k_ref[t]
            st_ref[i] = s
            y_ref[t, pl.ds(i, 1), :] = jnp.sum(s * r_ref[t], axis=0, keepdims=True)
        return carry

    lax.fori_loop(0, tl, step, 0)

    @pl.when(l == pl.num_programs(1) - 1)
    def _():
        sfin_ref[...] = st_ref[...]


def _scan(w, a, b, k, r, v, s0):
    l, n, gl = w.shape
    nv = v.shape[1]
    tl = _row_tile(l, 64)
    rows = pl.BlockSpec((tl, n, LANES), lambda g, i: (i, 0, g))
    vals = pl.BlockSpec((tl, nv, LANES), lambda g, i: (i, 0, g))
    state = pl.BlockSpec((nv, n, LANES), lambda g, i: (0, 0, g))
    return pl.pallas_call(
        functools.partial(_scan_kernel, tl=tl, nv=nv),
        grid=(gl // LANES, l // tl),
        in_specs=[rows] * 5 + [vals, state],
        out_specs=[vals, state],
        out_shape=[jax.ShapeDtypeStruct((l, nv, gl), F32), jax.ShapeDtypeStruct((nv, n, gl), F32)],
        scratch_shapes=[pltpu.VMEM((nv, n, LANES), F32)],
        compiler_params=_cparams("parallel", "arbitrary"),
        name="rwkv_scan",
    )(w, a, b, k, r, v, s0)


def _cmlp_norm(pc, lng, lnb, bd, d_c):
    ge = _gelu(pc)
    u = ge[:, :d_c]
    v = ge[:, d_c:]
    mean = _group_sum(v, bd) * (1.0 / HEAD)
    d = v - mean
    var = _group_sum(d * d, bd) * (1.0 / HEAD)
    return u, d * lax.rsqrt(var + LN_EPS) * lng + lnb


def _cmlp_kernel(pc_ref, lng_ref, lnb_ref, ws_ref, bs_ref, bd_ref, yc_ref, *, d_c, n_chunks):
    u, vn = _cmlp_norm(pc_ref[...], lng_ref[...], lnb_ref[...], bd_ref[...], d_c)
    vb = vn.astype(BF16)
    ri = lax.broadcasted_iota(jnp.int32, (CHUNK, CHUNK), 0)
    ci = lax.broadcasted_iota(jnp.int32, (CHUNK, CHUNK), 1)
    lane = lax.broadcasted_iota(jnp.int32, (CHUNK, LANES), 1)
    n_groups = d_c // HEAD
    wsm = [jnp.where(ri >= ci, ws_ref[g], 0.0).astype(BF16) for g in range(n_groups)]
    for c in range(n_chunks):
        rows = slice(c * CHUNK, (c + 1) * CHUNK)
        for p in range(n_groups // 2):
            cols = slice(p * LANES, (p + 1) * LANES)
            vp = vb[rows, cols]
            s0 = jnp.dot(wsm[2 * p], vp, preferred_element_type=F32)
            s1 = jnp.dot(wsm[2 * p + 1], vp, preferred_element_type=F32)
            s = jnp.where(lane < HEAD, s0, s1) + bs_ref[:, cols]
            yc_ref[rows, cols] = u[rows, cols] * s


def _cmlp(pc, lng, lnb, ws, bs_rows, bd):
    t, p_c = pc.shape
    d_c = p_c // 2
    tm = _row_tile(t, 256)
    assert tm % CHUNK == 0
    return pl.pallas_call(
        functools.partial(_cmlp_kernel, d_c=d_c, n_chunks=tm // CHUNK),
        grid=(t // tm,),
        in_specs=[pl.BlockSpec((tm, p_c), lambda i: (i, 0)), _full((1, d_c)), _full((1, d_c)),
                  _full(ws.shape), _full(bs_rows.shape), _full(bd.shape)],
        out_specs=pl.BlockSpec((tm, d_c), lambda i: (i, 0)),
        out_shape=jax.ShapeDtypeStruct((t, d_c), F32),
        compiler_params=_cparams("parallel"),
        name="cmlp",
    )(pc, lng, lnb, ws, bs_rows, bd)


def _cmlp_first_kernel(pc_ref, lng_ref, lnb_ref, w00_ref, b0_ref, bd_ref, yc_ref, vn_ref, *, d_c):
    u, vn = _cmlp_norm(pc_ref[...], lng_ref[...], lnb_ref[...], bd_ref[...], d_c)
    yc_ref[...] = u * (w00_ref[...] * vn + b0_ref[...])
    vn_ref[...] = vn


def _cmlp_first(pc, lng, lnb, w00, b0, bd):
    t, p_c = pc.shape
    d_c = p_c // 2
    return pl.pallas_call(
        functools.partial(_cmlp_first_kernel, d_c=d_c),
        grid=(1,),
        in_specs=[_full(pc.shape), _full((1, d_c)), _full((1, d_c)), _full((1, d_c)), _full((1, d_c)),
                  _full(bd.shape)],
        out_specs=[_full((t, d_c))] * 2,
        out_shape=[jax.ShapeDtypeStruct((t, d_c), F32)] * 2,
        compiler_params=_cparams("arbitrary"),
        name="cmlp_first",
    )(pc, lng, lnb, w00, b0, bd)


def _mix_kernel(x_ref, y_ref, bonus_ref, g_ref, yc_ref, lng_ref, lnb_ref, bd_ref, wo_ref, o_ref, *, d_a):
    bd = bd_ref[...]
    y = y_ref[...]
    mean = _group_sum(y, bd) * (1.0 / HEAD)
    d = y - mean
    var = _group_sum(d * d, bd) * (1.0 / HEAD)
    yn = d * lax.rsqrt(var + GN_EPS) * lng_ref[...] + lnb_ref[...]
    ya = (yn + bonus_ref[...]) * g_ref[...]
    o_ref[...] = (x_ref[...]
                  + jnp.dot(ya.astype(BF16), wo_ref[0:d_a, :], preferred_element_type=F32)
                  + jnp.dot(yc_ref[...].astype(BF16), wo_ref[d_a:, :], preferred_element_type=F32))


def _mix(x, y, bonus, g, yc, lng, lnb, bd, wo_bf):
    t, d = x.shape
    d_a = y.shape[1]
    d_c = yc.shape[1]
    tm = _row_tile(t, 256)
    row = lambda n: pl.BlockSpec((tm, n), lambda i: (i, 0))
    return pl.pallas_call(
        functools.partial(_mix_kernel, d_a=d_a),
        grid=(t // tm,),
        in_specs=[row(d), row(d_a), row(d_a), row(d_a), row(d_c), _full((1, d_a)), _full((1, d_a)),
                  _full(bd.shape), _full(wo_bf.shape)],
        out_specs=row(d),
        out_shape=jax.ShapeDtypeStruct((t, d), F32),
        compiler_params=_cparams("parallel"),
        name="mix_out",
    )(x, y, bonus, g, yc, lng, lnb, bd, wo_bf)


def _peerq_kernel(x_ref, g_ref, wq_ref, sk_ref, xn_ref, st_ref, *, n_hc):
    hb = _rmsnorm(x_ref[...], g_ref[...]).astype(BF16)
    xn_ref[...] = hb
    qb = jnp.dot(hb, wq_ref[...], preferred_element_type=F32).astype(BF16)
    for hc in range(n_hc):
        st_ref[hc] = lax.dot_general(sk_ref[hc % 2], qb[:, hc * LANES:(hc + 1) * LANES],
                                     (((1,), (1,)), ((), ())), preferred_element_type=F32)


def _peerq(x, g, wq_bf, sk_bf):
    t, d = x.shape
    n_hc = wq_bf.shape[1] // LANES
    tm = _row_tile(t, 512)
    return pl.pallas_call(
        functools.partial(_peerq_kernel, n_hc=n_hc),
        grid=(t // tm,),
        in_specs=[pl.BlockSpec((tm, d), lambda i: (i, 0)), _full((1, d)), _full(wq_bf.shape),
                  _full(sk_bf.shape)],
        out_specs=[pl.BlockSpec((tm, d), lambda i: (i, 0)),
                   pl.BlockSpec((n_hc, N_KEYS, tm), lambda i: (0, 0, i))],
        out_shape=[jax.ShapeDtypeStruct((t, d), BF16), jax.ShapeDtypeStruct((n_hc, N_KEYS, t), F32)],
        compiler_params=_cparams("parallel"),
        name="peer_query",
    )(x, g, wq_bf, sk_bf)


def _top_values(s, n):
    n_pad = -(-n // 8) * 8
    row = lax.broadcasted_iota(jnp.int32, (n_pad, s.shape[1]), 0)
    out = jnp.full((n_pad, s.shape[1]), NEG_INF, F32)
    cur = s
    for j in range(n):
        m = jnp.max(cur, axis=0, keepdims=True)
        out = jnp.where(row == j, m, out)
        cur = jnp.where(cur == m, NEG_INF, cur)
    return out


def _peer_topk_kernel(st_ref, p1_ref, p2_ref, th_ref, *, n_heads):
    k = PEER_TOPK

    def head(h, carry):
        s1 = st_ref[2 * h]
        s2 = st_ref[2 * h + 1]
        v1 = _top_values(s1, k + 1)
        v2 = _top_values(s2, k + 1)
        row = lax.broadcasted_iota(jnp.int32, v2.shape, 0)
        cands = []
        for i in range(k + 1):
            n_j = (k + 1) // (i + 1)
            rows = v2 if n_j > 8 else v2[0:8]
            cands.append(jnp.where(row[0:rows.shape[0]] < n_j, v1[i:i + 1] + rows, NEG_INF))
        cand = jnp.concatenate(cands, axis=0)
        top = _top_values(cand, k + 1)
        m = top[0:1]
        c_k = top[k - 1:k]
        z = jnp.sum(jnp.where(cand >= c_k, jnp.exp(cand - m), 0.0), axis=0, keepdims=True)
        inv_z = 1.0 / z
        th_ref[pl.ds(h, 1), :] = jnp.exp(0.5 * (c_k + top[k:k + 1]) - m) * inv_z
        p1_ref[h] = jnp.exp(s1 - v1[0:1]) * inv_z
        p2_ref[h] = jnp.exp(s2 - v2[0:1])
        return carry

    lax.fori_loop(0, n_heads, head, 0)


def _peer_topk(st):
    n_hc, n, t = st.shape
    n_heads = n_hc // 2
    tk = LANES
    fac = pl.BlockSpec((n_heads, n, tk), lambda i: (0, 0, i))
    return pl.pallas_call(
        functools.partial(_peer_topk_kernel, n_heads=n_heads),
        grid=(t // tk,),
        in_specs=[pl.BlockSpec((n_hc, n, tk), lambda i: (0, 0, i))],
        out_specs=[fac, fac, pl.BlockSpec((n_heads, tk), lambda i: (0, i))],
        out_shape=[jax.ShapeDtypeStruct((n_heads, n, t), F32)] * 2 + [jax.ShapeDtypeStruct((n_heads, t), F32)],
        compiler_params=_cparams("parallel"),
        name="peer_topk",
    )(st)


ROW_BLK = 32
KEYS_PER_TILE = 8


def _peer_dense_kernel(xn_ref, x_ref, u_ref, vt_ref, p1_ref, p2_ref, th_ref, o_ref, ht_ref, at_ref, acc_ref,
                       *, n_heads, tm):
    j = pl.program_id(1)

    @pl.when(j == 0)
    def _():
        acc_ref[...] = jnp.zeros_like(acc_ref)

    ht_ref[...] = lax.dot_general(u_ref[...], xn_ref[...], (((1,), (1,)), ((), ())),
                                  preferred_element_type=F32)

    for k in range(KEYS_PER_TILE):
        for lc in range(tm // LANES):
            cols = pl.ds(lc * LANES, LANES)
            p1_rows = [p1_ref[h, k:k + 1, cols] for h in range(n_heads)]
            th_rows = [th_ref[h:h + 1, cols] for h in range(n_heads)]
            for rb in range(N_KEYS // ROW_BLK):
                r2 = pl.ds(rb * ROW_BLK, ROW_BLK)
                r0 = pl.ds(k * N_KEYS + rb * ROW_BLK, ROW_BLK)
                gsum = jnp.zeros((ROW_BLK, LANES), F32)
                for h in range(n_heads):
                    w = p1_rows[h] * p2_ref[h, r2, cols]
                    gsum = gsum + jnp.where(w >= th_rows[h], w, 0.0)
                at_ref[r0, cols] = (_gelu(ht_ref[r0, cols]) * gsum).astype(BF16)

    acc_ref[...] += jnp.dot(vt_ref[...], at_ref[...], preferred_element_type=F32)

    @pl.when(j == pl.num_programs(1) - 1)
    def _():
        o_ref[...] = x_ref[...] + acc_ref[...].T


def _peer_dense(xn, x, u_bf, vt_bf, p1, p2, th):
    t, d = x.shape
    n_exp = u_bf.shape[0]
    n_heads = p1.shape[0]
    tm = _row_tile(t, 512)
    te = KEYS_PER_TILE * N_KEYS
    assert n_exp == N_KEYS * N_KEYS
    return pl.pallas_call(
        functools.partial(_peer_dense_kernel, n_heads=n_heads, tm=tm),
        grid=(t // tm, n_exp // te),
        in_specs=[pl.BlockSpec((tm, d), lambda i, j: (i, 0)),
                  pl.BlockSpec((tm, d), lambda i, j: (i, 0)),
                  pl.BlockSpec((te, d), lambda i, j: (j, 0)),
                  pl.BlockSpec((d, te), lambda i, j: (0, j)),
                  pl.BlockSpec((n_heads, KEYS_PER_TILE, tm), lambda i, j: (0, j, i)),
                  pl.BlockSpec((n_heads, N_KEYS, tm), lambda i, j: (0, 0, i)),
                  pl.BlockSpec((n_heads, tm), lambda i, j: (0, i))],
        out_specs=pl.BlockSpec((tm, d), lambda i, j: (i, 0)),
        out_shape=jax.ShapeDtypeStruct((t, d), F32),
        scratch_shapes=[pltpu.VMEM((te, tm), F32), pltpu.VMEM((te, tm), BF16), pltpu.VMEM((d, tm), F32)],
        compiler_params=_cparams("parallel", "arbitrary"),
        name="peer_dense",
    )(xn, x, u_bf, vt_bf, p1, p2, th)


def _ple_kernel(x_ref, p_ref, g_ref, gw_ref, pw_ref, gf_ref, o_ref, *, final):
    x = x_ref[...]
    hn = _rmsnorm(x, g_ref[...])
    gate = jax.nn.sigmoid(jnp.dot(hn.astype(BF16), gw_ref[...], preferred_element_type=F32))
    e = jnp.dot(p_ref[...].astype(BF16), pw_ref[...], preferred_element_type=F32)
    x = x + e * gate
    o_ref[...] = _rmsnorm(x, gf_ref[...]) if final else x


def _ple(x, p, g, gw_bf, pw_bf, gf, final):
    t, d = x.shape
    dp = p.shape[1]
    tm = _row_tile(t, 512)
    return pl.pallas_call(
        functools.partial(_ple_kernel, final=final),
        grid=(t // tm,),
        in_specs=[pl.BlockSpec((tm, d), lambda i: (i, 0)), pl.BlockSpec((tm, dp), lambda i: (i, 0)),
                  _full((1, d)), _full(gw_bf.shape), _full(pw_bf.shape), _full((1, d))],
        out_specs=pl.BlockSpec((tm, d), lambda i: (i, 0)),
        out_shape=jax.ShapeDtypeStruct((t, d), F32),
        compiler_params=_cparams("parallel"),
        name="ple",
    )(x, p, g, gw_bf, pw_bf, gf)


def _rwkv_prompt_scan(parts, batch, seq, n_heads):
    bh = batch * n_heads
    assert 2 * bh == LANES

    def rows(x):
        y = x.reshape(batch, seq, n_heads, HEAD).transpose(1, 3, 0, 2).reshape(seq, HEAD, bh)
        return jnp.concatenate([y, y], axis=-1)

    r, k, v, w, a, b = parts
    vt = v.reshape(batch, seq, n_heads, 2, HEAD // 2).transpose(1, 4, 3, 0, 2).reshape(seq, HEAD // 2, LANES)
    s0 = jnp.zeros((HEAD // 2, HEAD, LANES), F32)
    y, s_fin = _scan(rows(w), rows(a), rows(b), rows(k), rows(r), vt, s0)
    y = y.reshape(seq, HEAD // 2, 2, batch, n_heads).transpose(3, 0, 4, 2, 1).reshape(batch * seq, n_heads * HEAD)
    s_fin = s_fin.reshape(HEAD // 2, HEAD, 2, batch, n_heads).transpose(3, 4, 2, 0, 1)
    return y, s_fin.reshape(batch, n_heads, HEAD, HEAD)


def _rwkv_sample_scan(parts, wkv_prev, batch, n_heads):
    bh = batch * n_heads
    assert bh % LANES == 0
    rows = lambda x: x.reshape(batch, n_heads, HEAD).transpose(2, 0, 1).reshape(1, HEAD, bh)
    r, k, v, w, a, b = parts
    s0 = wkv_prev.transpose(2, 3, 0, 1).reshape(HEAD, HEAD, bh)
    y, s_fin = _scan(rows(w), rows(a), rows(b), rows(k), rows(r), rows(v), s0)
    y = y.reshape(HEAD, batch, n_heads).transpose(1, 2, 0).reshape(batch, n_heads * HEAD)
    return y, s_fin.reshape(HEAD, HEAD, batch, n_heads).transpose(2, 3, 0, 1)


def _layer(x, p, wkv_prev, shift_prev, lw, gf, final):
    batch, seq, d = x.shape
    t = batch * seq
    d_a = lw["w0"].shape[1]
    n_heads = d_a // HEAD
    p_a = lw["mu"].shape[1]
    xt = x.reshape(t, d)

    pa, pc = _proj(xt, lw["norm_mix_g"], lw["w_in"], p_a)
    pa3 = pa.reshape(batch, seq, p_a)
    prev = jnp.concatenate([shift_prev[:, None, :], pa3[:, :-1]], axis=1).reshape(t, p_a)
    r, k, v, w, a, b, g, bonus = _prep(pa, prev, lw["mu"], lw["w0"], lw["a0"], lw["wa2"], lw["g2"],
                                       lw["k_k"], lw["k_a"], lw["r_k"], lw["bd"])
    if wkv_prev is None:
        y, wkv_new = _rwkv_prompt_scan((r, k, v, w, a, b), batch, seq, n_heads)
        yc = _cmlp(pc, lw["cmlp_ln_g"], lw["cmlp_ln_b"], lw["cmlp_ws"], lw["cmlp_bs_rows"], lw["bd"])
        v_rows = None
    else:
        assert seq == 1
        y, wkv_new = _rwkv_sample_scan((r, k, v, w, a, b), wkv_prev, batch, n_heads)
        yc, v_rows = _cmlp_first(pc, lw["cmlp_ln_g"], lw["cmlp_ln_b"], lw["cmlp_w00"], lw["cmlp_b0"], lw["bd"])
        v_rows = v_rows.reshape(batch, seq, -1)
    x1 = _mix(xt, y, bonus, g, yc, lw["rwkv_ln_g"], lw["rwkv_ln_b"], lw["bd"], lw["w_out"])

    xn, st = _peerq(x1, lw["norm_ffn_g"], lw["peer_wq"], lw["peer_subkeys"])
    p1, p2, th = _peer_topk(st)
    x2 = _peer_dense(xn, x1, lw["peer_u"], lw["peer_vt"], p1, p2, th)

    x3 = _ple(x2, p.reshape(t, -1), lw["norm_ple_g"], lw["ple_gate_w"], lw["ple_w"], gf, final)
    return x3.reshape(batch, seq, d), wkv_new, pa3[:, -1], v_rows


def kernel(x_prompt, x_sample, state_wkv, state_shift, p_prompt, p_sample, norm_mix_g, w_in, shift_mu, rwkv_w0, rwkv_w2, rwkv_a0, rwkv_a2, rwkv_g2, rwkv_k_k, rwkv_k_a, rwkv_r_k, rwkv_ln_g, rwkv_ln_b, cmlp_ln_g, cmlp_ln_b, cmlp_ws, cmlp_bs, w_out, norm_ffn_g, peer_wq, peer_subkeys, peer_u, peer_v, norm_ple_g, ple_w, ple_gate_w, norm_final_g):
    depth = state_wkv.shape[0]
    batch = x_prompt.shape[0]
    d_a = rwkv_w0.shape[1]
    d_c = cmlp_ln_g.shape[1]
    p_a = shift_mu.shape[1]
    lora_w = rwkv_w2.shape[1]
    lora_a = rwkv_a2.shape[1]
    assert lora_w == HEAD and lora_a == HEAD and d_a == d_c
    gidx = jnp.arange(d_a) // HEAD
    bd = (gidx[:, None] == gidx[None, :]).astype(F32)
    row = lambda z: z.reshape(1, -1)
    gf = row(norm_final_g)

    hp, hs = x_prompt, x_sample
    outs = [[] for _ in range(5)]
    for i in range(depth):
        wa2 = jnp.zeros((lora_w + lora_a, 2 * d_a), F32)
        wa2 = wa2.at[:lora_w, :d_a].set(rwkv_w2[i]).at[lora_w:, d_a:].set(rwkv_a2[i])
        lw = dict(
            norm_mix_g=row(norm_mix_g[i]), w_in=w_in[i].astype(BF16), mu=row(shift_mu[i]),
            w0=row(rwkv_w0[i]), a0=row(rwkv_a0[i]), wa2=wa2.astype(BF16), g2=rwkv_g2[i].astype(BF16),
            k_k=row(rwkv_k_k[i]), k_a=row(rwkv_k_a[i]), r_k=row(rwkv_r_k[i]), bd=bd,
            rwkv_ln_g=row(rwkv_ln_g[i]), rwkv_ln_b=row(rwkv_ln_b[i]),
            cmlp_ln_g=row(cmlp_ln_g[i]), cmlp_ln_b=row(cmlp_ln_b[i]), cmlp_ws=cmlp_ws[i],
            cmlp_bs_rows=jnp.repeat(cmlp_bs[i].T, HEAD, axis=1),
            cmlp_w00=row(jnp.repeat(cmlp_ws[i][:, 0, 0], HEAD)), cmlp_b0=row(jnp.repeat(cmlp_bs[i][:, 0], HEAD)),
            w_out=w_out[i].astype(BF16), norm_ffn_g=row(norm_ffn_g[i]), peer_wq=peer_wq[i].astype(BF16),
            peer_subkeys=peer_subkeys[i].astype(BF16), peer_u=peer_u[i].astype(BF16),
            peer_vt=peer_v[i].astype(BF16).T, norm_ple_g=row(norm_ple_g[i]),
            ple_w=ple_w[i].astype(BF16), ple_gate_w=ple_gate_w[i].astype(BF16),
        )
        final = i == depth - 1
        shift0 = jnp.zeros((batch, p_a), F32)
        hp, wkv_p, shift_p, _ = _layer(hp, p_prompt[i], None, shift0, lw, gf, final)
        hs, wkv_s, shift_s, v_s = _layer(hs, p_sample[i], state_wkv[i], state_shift[i], lw, gf, final)
        for o, val in zip(outs, (wkv_p, shift_p, wkv_s, shift_s, v_s)):
            o.append(val)
    return (hp, hs) + tuple(jnp.stack(o) for o in outs)
```

```python
import functools
import math

import jax
import jax.numpy as jnp
from jax import lax
from jax.experimental import pallas as pl
from jax.experimental.pallas import tpu as pltpu

F32 = jnp.float32
BF16 = jnp.bfloat16
HIGHEST = lax.Precision.HIGHEST

LANES = 128
HEAD = 64
CHUNK = 128
N_KEYS = 128
PEER_TOPK = 16
RMS_EPS = 1e-6
LN_EPS = 1e-5
GN_EPS = 64e-5
SQRT_HALF = math.sqrt(0.5)
NEG_INF = float("-inf")
VMEM_LIMIT = 56 * 1024 * 1024


def _cparams(*sem):
    return pltpu.CompilerParams(dimension_semantics=sem, vmem_limit_bytes=VMEM_LIMIT)


def _rmsnorm(x, g):
    return x * lax.rsqrt(jnp.mean(x * x, axis=-1, keepdims=True) + RMS_EPS) * g


def _gelu(x):
    return 0.5 * x * (1.0 + lax.erf(x * SQRT_HALF))


def _group_sum(x, bd):
    return jnp.dot(x, bd, precision=HIGHEST, preferred_element_type=F32)


def _row_tile(n, want):
    t = min(n, want)
    assert n % t == 0, (n, t)
    return t


def _full(shape):
    return pl.BlockSpec(shape, lambda *_: (0,) * len(shape))


def _proj_kernel(x_ref, g_ref, w_ref, pa_ref, pc_ref, *, p_a):
    h = _rmsnorm(x_ref[...], g_ref[...])
    p = jnp.dot(h.astype(BF16), w_ref[...], preferred_element_type=F32)
    pa_ref[...] = p[:, :p_a]
    pc_ref[...] = p[:, p_a:]


def _proj(x, g, w_bf, p_a):
    t, d = x.shape
    p_tot = w_bf.shape[1]
    tm = _row_tile(t, 512)
    return pl.pallas_call(
        functools.partial(_proj_kernel, p_a=p_a),
        grid=(t // tm,),
        in_specs=[pl.BlockSpec((tm, d), lambda i: (i, 0)), _full((1, d)), _full((d, p_tot))],
        out_specs=[pl.BlockSpec((tm, p_a), lambda i: (i, 0)),
                   pl.BlockSpec((tm, p_tot - p_a), lambda i: (i, 0))],
        out_shape=[jax.ShapeDtypeStruct((t, p_a), F32), jax.ShapeDtypeStruct((t, p_tot - p_a), F32)],
        compiler_params=_cparams("parallel"),
        name="proj",
    )(x, g, w_bf)


def _prep_kernel(pa_ref, prev_ref, mu_ref, w0_ref, a0_ref, wa2_ref, g2_ref, kk_ref, ka_ref, rk_ref, bd_ref,
                 r_o, k_o, v_o, w_o, a_o, b_o, g_o, bonus_o, *, d_a):
    pa = pa_ref[...]
    xm = pa + (prev_ref[...] - pa) * mu_ref[...]
    r = xm[:, 0:d_a]
    k = xm[:, d_a:2 * d_a]
    v = xm[:, 2 * d_a:3 * d_a]
    wal = xm[:, 3 * d_a:3 * d_a + LANES]
    gl = xm[:, 3 * d_a + LANES:]
    lane = lax.broadcasted_iota(jnp.int32, wal.shape, 1)
    wal = jnp.where(lane < HEAD, jnp.tanh(wal), wal)
    lo = jnp.dot(wal.astype(BF16), wa2_ref[...], preferred_element_type=F32)
    z = -(w0_ref[...] + lo[:, :d_a])
    softplus = jnp.maximum(z, 0.0) + jnp.log1p(jnp.exp(-jnp.abs(z)))
    decay = jnp.exp(-jnp.exp(-softplus - 0.5))
    a = jax.nn.sigmoid(a0_ref[...] + lo[:, d_a:])
    g = jnp.dot(jax.nn.sigmoid(gl).astype(BF16), g2_ref[...], preferred_element_type=F32)
    bd = bd_ref[...]
    kk = k * kk_ref[...]
    kk = kk / jnp.maximum(jnp.sqrt(_group_sum(kk * kk, bd)), 1e-12)
    kh = k * (1.0 + (a - 1.0) * ka_ref[...])
    r_o[...] = r
    k_o[...] = kh
    v_o[...] = v
    w_o[...] = decay
    a_o[...] = -kk
    b_o[...] = kk * a
    g_o[...] = g
    bonus_o[...] = _group_sum(r * kh * rk_ref[...], bd) * v


def _prep(pa, prev, mu, w0, a0, wa2_bf, g2_bf, k_k, k_a, r_k, bd):
    t, p_a = pa.shape
    d_a = w0.shape[1]
    tm = _row_tile(t, 256)
    row = lambda n: pl.BlockSpec((tm, n), lambda i: (i, 0))
    return pl.pallas_call(
        functools.partial(_prep_kernel, d_a=d_a),
        grid=(t // tm,),
        in_specs=[row(p_a), row(p_a), _full((1, p_a)), _full((1, d_a)), _full((1, d_a)),
                  _full(wa2_bf.shape), _full(g2_bf.shape), _full((1, d_a)), _full((1, d_a)),
                  _full((1, d_a)), _full(bd.shape)],
        out_specs=[row(d_a)] * 8,
        out_shape=[jax.ShapeDtypeStruct((t, d_a), F32)] * 8,
        compiler_params=_cparams("parallel"),
        name="rwkv_prep",
    )(pa, prev, mu, w0, a0, wa2_bf, g2_bf, k_k, k_a, r_k, bd)


def _scan_kernel(w_ref, a_ref, b_ref, k_ref, r_ref, v_ref, s0_ref, y_ref, sfin_ref, st_ref, *, tl, nv):
    l = pl.program_id(1)

    @pl.when(l == 0)
    def _():
        st_ref[...] = s0_ref[...]

    def step(t, carry):
        for i in range(nv):
            s = st_ref[i]
            sa = jnp.sum(s * a_ref[t], axis=0, keepdims=True)
            s = s * w_ref[t] + sa * b_ref[t] + v_ref[t, pl.ds(i, 1), :] * k_ref[t]
            st_ref[i] = s
            y_ref[t, pl.ds(i, 1), :] = jnp.sum(s * r_ref[t], axis=0, keepdims=True)
        return carry

    lax.fori_loop(0, tl, step, 0)

    @pl.when(l == pl.num_programs(1) - 1)
    def _():
        sfin_ref[...] = st_ref[...]


def _scan(w, a, b, k, r, v, s0):
    l, n, gl = w.shape
    nv = v.shape[1]
    tl = _row_tile(l, 64)
    rows = pl.BlockSpec((tl, n, LANES), lambda g, i: (i, 0, g))
    vals = pl.BlockSpec((tl, nv, LANES), lambda g, i: (i, 0, g))
    state = pl.BlockSpec((nv, n, LANES), lambda g, i: (0, 0, g))
    return pl.pallas_call(
        functools.partial(_scan_kernel, tl=tl, nv=nv),
        grid=(gl // LANES, l // tl),
        in_specs=[rows] * 5 + [vals, state],
        out_specs=[vals, state],
        out_shape=[jax.ShapeDtypeStruct((l, nv, gl), F32), jax.ShapeDtypeStruct((nv, n, gl), F32)],
        scratch_shapes=[pltpu.VMEM((nv, n, LANES), F32)],
        compiler_params=_cparams("parallel", "arbitrary"),
        name="rwkv_scan",
    )(w, a, b, k, r, v, s0)


def _cmlp_norm(pc, lng, lnb, bd, d_c):
    ge = _gelu(pc)
    u = ge[:, :d_c]
    v = ge[:, d_c:]
    mean = _group_sum(v, bd) * (1.0 / HEAD)
    d = v - mean
    var = _group_sum(d * d, bd) * (1.0 / HEAD)
    return u, d * lax.rsqrt(var + LN_EPS) * lng + lnb


def _cmlp_kernel(pc_ref, lng_ref, lnb_ref, ws_ref, bs_ref, bd_ref, yc_ref, *, d_c, n_chunks):
    u, vn = _cmlp_norm(pc_ref[...], lng_ref[...], lnb_ref[...], bd_ref[...], d_c)
    vb = vn.astype(BF16)
    ri = lax.broadcasted_iota(jnp.int32, (CHUNK, CHUNK), 0)
    ci = lax.broadcasted_iota(jnp.int32, (CHUNK, CHUNK), 1)
    lane = lax.broadcasted_iota(jnp.int32, (CHUNK, LANES), 1)
    n_groups = d_c // HEAD
    wsm = [jnp.where(ri >= ci, ws_ref[g], 0.0).astype(BF16) for g in range(n_groups)]
    for c in range(n_chunks):
        rows = slice(c * CHUNK, (c + 1) * CHUNK)
        for p in range(n_groups // 2):
            cols = slice(p * LANES, (p + 1) * LANES)
            vp = vb[rows, cols]
            s0 = jnp.dot(wsm[2 * p], vp, preferred_element_type=F32)
            s1 = jnp.dot(wsm[2 * p + 1], vp, preferred_element_type=F32)
            s = jnp.where(lane < HEAD, s0, s1) + bs_ref[:, cols]
            yc_ref[rows, cols] = u[rows, cols] * s


def _cmlp(pc, lng, lnb, ws, bs_rows, bd):
    t, p_c = pc.shape
    d_c = p_c // 2
    tm = _row_tile(t, 256)
    assert tm % CHUNK == 0
    return pl.pallas_call(
        functools.partial(_cmlp_kernel, d_c=d_c, n_chunks=tm // CHUNK),
        grid=(t // tm,),
        in_specs=[pl.BlockSpec((tm, p_c), lambda i: (i, 0)), _full((1, d_c)), _full((1, d_c)),
                  _full(ws.shape), _full(bs_rows.shape), _full(bd.shape)],
        out_specs=pl.BlockSpec((tm, d_c), lambda i: (i, 0)),
        out_shape=jax.ShapeDtypeStruct((t, d_c), F32),
        compiler_params=_cparams("parallel"),
        name="cmlp",
    )(pc, lng, lnb, ws, bs_rows, bd)


def _cmlp_first_kernel(pc_ref, lng_ref, lnb_ref, w00_ref, b0_ref, bd_ref, yc_ref, vn_ref, *, d_c):
    u, vn = _cmlp_norm(pc_ref[...], lng_ref[...], lnb_ref[...], bd_ref[...], d_c)
    yc_ref[...] = u * (w00_ref[...] * vn + b0_ref[...])
    vn_ref[...] = vn


def _cmlp_first(pc, lng, lnb, w00, b0, bd):
    t, p_c = pc.shape
    d_c = p_c // 2
    return pl.pallas_call(
        functools.partial(_cmlp_first_kernel, d_c=d_c),
        grid=(1,),
        in_specs=[_full(pc.shape), _full((1, d_c)), _full((1, d_c)), _full((1, d_c)), _full((1, d_c)),
                  _full(bd.shape)],
        out_specs=[_full((t, d_c))] * 2,
        out_shape=[jax.ShapeDtypeStruct((t, d_c), F32)] * 2,
        compiler_params=_cparams("arbitrary"),
        name="cmlp_first",
    )(pc, lng, lnb, w00, b0, bd)


def _mix_kernel(x_ref, y_ref, bonus_ref, g_ref, yc_ref, lng_ref, lnb_ref, bd_ref, wo_ref, o_ref, *, d_a):
    bd = bd_ref[...]
    y = y_ref[...]
    mean = _group_sum(y, bd) * (1.0 / HEAD)
    d = y - mean
    var = _group_sum(d * d, bd) * (1.0 / HEAD)
    yn = d * lax.rsqrt(var + GN_EPS) * lng_ref[...] + lnb_ref[...]
    ya = (yn + bonus_ref[...]) * g_ref[...]
    o_ref[...] = (x_ref[...]
                  + jnp.dot(ya.astype(BF16), wo_ref[0:d_a, :], preferred_element_type=F32)
                  + jnp.dot(yc_ref[...].astype(BF16), wo_ref[d_a:, :], preferred_element_type=F32))


def _mix(x, y, bonus, g, yc, lng, lnb, bd, wo_bf):
    t, d = x.shape
    d_a = y.shape[1]
    d_c = yc.shape[1]
    tm = _row_tile(t, 256)
    row = lambda n: pl.BlockSpec((tm, n), lambda i: (i, 0))
    return pl.pallas_call(
        functools.partial(_mix_kernel, d_a=d_a),
        grid=(t // tm,),
        in_specs=[row(d), row(d_a), row(d_a), row(d_a), row(d_c), _full((1, d_a)), _full((1, d_a)),
                  _full(bd.shape), _full(wo_bf.shape)],
        out_specs=row(d),
        out_shape=jax.ShapeDtypeStruct((t, d), F32),
        compiler_params=_cparams("parallel"),
        name="mix_out",
    )(x, y, bonus, g, yc, lng, lnb, bd, wo_bf)


def _peerq_kernel(x_ref, g_ref, wq_ref, sk_ref, xn_ref, st_ref, *, n_hc):
    hb = _rmsnorm(x_ref[...], g_ref[...]).astype(BF16)
    xn_ref[...] = hb
    qb = jnp.dot(hb, wq_ref[...], preferred_element_type=F32).astype(BF16)
    for hc in range(n_hc):
        st_ref[hc] = lax.dot_general(sk_ref[hc % 2], qb[:, hc * LANES:(hc + 1) * LANES],
                                     (((1,), (1,)), ((), ())), preferred_element_type=F32)


def _peerq(x, g, wq_bf, sk_bf):
    t, d = x.shape
    n_hc = wq_bf.shape[1] // LANES
    tm = _row_tile(t, 512)
    return pl.pallas_call(
        functools.partial(_peerq_kernel, n_hc=n_hc),
        grid=(t // tm,),
        in_specs=[pl.BlockSpec((tm, d), lambda i: (i, 0)), _full((1, d)), _full(wq_bf.shape),
                  _full(sk_bf.shape)],
        out_specs=[pl.BlockSpec((tm, d), lambda i: (i, 0)),
                   pl.BlockSpec((n_hc, N_KEYS, tm), lambda i: (0, 0, i))],
        out_shape=[jax.ShapeDtypeStruct((t, d), BF16), jax.ShapeDtypeStruct((n_hc, N_KEYS, t), F32)],
        compiler_params=_cparams("parallel"),
        name="peer_query",
    )(x, g, wq_bf, sk_bf)


def _batcher_network(n):
    pairs = []

    def merge(lo, m, r):
        step = 2 * r
        if step < m:
            merge(lo, m, step)
            merge(lo + r, m, step)
            pairs.extend((i, i + r) for i in range(lo + r, lo + m - r, step))
        else:
            pairs.append((lo, lo + r))

    def sort(lo, m):
        if m > 1:
            sort(lo, m // 2)
            sort(lo + m // 2, m // 2)
            merge(lo, m, 1)

    sort(0, n)
    return pairs


SUBLANES = 8
KEY_SLABS = N_KEYS // SUBLANES
SLAB_NETWORK = _batcher_network(KEY_SLABS)


def _pop_heads(lists, extra_heads, n):
    lists, extra_heads = list(lists), list(extra_heads)
    tops = []
    for r in range(n):
        head = lists[0]
        for e in extra_heads:
            head = jnp.maximum(head, e)
        m = jnp.max(head, axis=0, keepdims=True)
        tops.append(m)
        hit = lists[0] == m
        for d in range(min(n - 1 - r, len(lists))):
            nxt = lists[d + 1] if d + 1 < len(lists) else NEG_INF
            lists[d] = jnp.where(hit, nxt, lists[d])
        if r < n - 1:
            extra_heads = [jnp.where(e == m, NEG_INF, e) for e in extra_heads]
    return tops


def _top_sorted(s, n):
    slabs = [s[SUBLANES * r:SUBLANES * (r + 1)] for r in range(KEY_SLABS)]
    for i, j in SLAB_NETWORK:
        slabs[i], slabs[j] = jnp.maximum(slabs[i], slabs[j]), jnp.minimum(slabs[i], slabs[j])
    return _pop_heads(slabs, [], n)


def _peer_topk_kernel(st_ref, p1_ref, p2_ref, th_ref, *, n_heads):
    k = PEER_TOPK
    n = k + 1
    n_pad = -(-n // SUBLANES) * SUBLANES

    def head(h, carry):
        s1 = st_ref[2 * h]
        s2 = st_ref[2 * h + 1]
        t1 = _top_sorted(s1, n)
        t2 = _top_sorted(s2, n)
        row = lax.broadcasted_iota(jnp.int32, (n_pad, s1.shape[1]), 0)
        v1 = jnp.full((n_pad, s1.shape[1]), NEG_INF, F32)
        for i in range(n):
            v1 = jnp.where(row == i, t1[i], v1)
        row8 = row[0:SUBLANES]
        lists = [v1[0:SUBLANES] + t2[0]]
        for j in range(1, n):
            lists.append(jnp.where(row8 < n // (j + 1), v1[0:SUBLANES] + t2[j], NEG_INF))
        singles = [v1[SUBLANES * b:SUBLANES * (b + 1)] + t2[0] for b in range(1, n_pad // SUBLANES)]
        c = _pop_heads(lists, singles, n)
        m = c[0]
        z = jnp.ones_like(m)
        for r in range(1, k):
            z = z + jnp.exp(c[r] - m)
        inv_z = 1.0 / z
        th_ref[pl.ds(h, 1), :] = jnp.exp(0.5 * (c[k - 1] + c[k]) - m) * inv_z
        p1_ref[h] = jnp.exp(s1 - t1[0]) * inv_z
        p2_ref[h] = jnp.exp(s2 - t2[0])
        return carry

    lax.fori_loop(0, n_heads, head, 0)


def _peer_topk(st):
    n_hc, n, t = st.shape
    n_heads = n_hc // 2
    tk = LANES
    fac = pl.BlockSpec((n_heads, n, tk), lambda i: (0, 0, i))
    return pl.pallas_call(
        functools.partial(_peer_topk_kernel, n_heads=n_heads),
        grid=(t // tk,),
        in_specs=[pl.BlockSpec((n_hc, n, tk), lambda i: (0, 0, i))],
        out_specs=[fac, fac, pl.BlockSpec((n_heads, tk), lambda i: (0, i))],
        out_shape=[jax.ShapeDtypeStruct((n_heads, n, t), F32)] * 2 + [jax.ShapeDtypeStruct((n_heads, t), F32)],
        compiler_params=_cparams("parallel"),
        name="peer_topk",
    )(st)


ROW_BLK = 32
KEYS_PER_TILE = 8


def _peer_dense_kernel(xn_ref, x_ref, u_ref, vt_ref, p1_ref, p2_ref, th_ref, o_ref, ht_ref, at_ref, acc_ref,
                       *, n_heads, tm):
    j = pl.program_id(1)

    @pl.when(j == 0)
    def _():
        acc_ref[...] = jnp.zeros_like(acc_ref)

    ht_ref[...] = lax.dot_general(u_ref[...], xn_ref[...], (((1,), (1,)), ((), ())),
                                  preferred_element_type=F32)

    for k in range(KEYS_PER_TILE):
        for lc in range(tm // LANES):
            cols = pl.ds(lc * LANES, LANES)
            p1_rows = [p1_ref[h, k:k + 1, cols] for h in range(n_heads)]
            th_rows = [th_ref[h:h + 1, cols] for h in range(n_heads)]
            for rb in range(N_KEYS // ROW_BLK):
                r2 = pl.ds(rb * ROW_BLK, ROW_BLK)
                r0 = pl.ds(k * N_KEYS + rb * ROW_BLK, ROW_BLK)
                gsum = jnp.zeros((ROW_BLK, LANES), F32)
                for h in range(n_heads):
                    w = p1_rows[h] * p2_ref[h, r2, cols]
                    gsum = gsum + jnp.where(w >= th_rows[h], w, 0.0)
                at_ref[r0, cols] = (_gelu(ht_ref[r0, cols]) * gsum).astype(BF16)

    acc_ref[...] += jnp.dot(vt_ref[...], at_ref[...], preferred_element_type=F32)

    @pl.when(j == pl.num_programs(1) - 1)
    def _():
        o_ref[...] = x_ref[...] + acc_ref[...].T


def _peer_dense(xn, x, u_bf, vt_bf, p1, p2, th):
    t, d = x.shape
    n_exp = u_bf.shape[0]
    n_heads = p1.shape[0]
    tm = _row_tile(t, 512)
    te = KEYS_PER_TILE * N_KEYS
    assert n_exp == N_KEYS * N_KEYS
    return pl.pallas_call(
        functools.partial(_peer_dense_kernel, n_heads=n_heads, tm=tm),
        grid=(t // tm, n_exp // te),
        in_specs=[pl.BlockSpec((tm, d), lambda i, j: (i, 0)),
                  pl.BlockSpec((tm, d), lambda i, j: (i, 0)),
                  pl.BlockSpec((te, d), lambda i, j: (j, 0)),
                  pl.BlockSpec((d, te), lambda i, j: (0, j)),
                  pl.BlockSpec((n_heads, KEYS_PER_TILE, tm), lambda i, j: (0, j, i)),
                  pl.BlockSpec((n_heads, N_KEYS, tm), lambda i, j: (0, 0, i)),
                  pl.BlockSpec((n_heads, tm), lambda i, j: (0, i))],
        out_specs=pl.BlockSpec((tm, d), lambda i, j: (i, 0)),
        out_shape=jax.ShapeDtypeStruct((t, d), F32),
        scratch_shapes=[pltpu.VMEM((te, tm), F32), pltpu.VMEM((te, tm), BF16), pltpu.VMEM((d, tm), F32)],
        compiler_params=_cparams("parallel", "arbitrary"),
        name="peer_dense",
    )(xn, x, u_bf, vt_bf, p1, p2, th)


def _ple_kernel(x_ref, p_ref, g_ref, gw_ref, pw_ref, gf_ref, o_ref, *, final):
    x = x_ref[...]
    hn = _rmsnorm(x, g_ref[...])
    gate = jax.nn.sigmoid(jnp.dot(hn.astype(BF16), gw_ref[...], preferred_element_type=F32))
    e = jnp.dot(p_ref[...].astype(BF16), pw_ref[...], preferred_element_type=F32)
    x = x + e * gate
    o_ref[...] = _rmsnorm(x, gf_ref[...]) if final else x


def _ple(x, p, g, gw_bf, pw_bf, gf, final):
    t, d = x.shape
    dp = p.shape[1]
    tm = _row_tile(t, 512)
    return pl.pallas_call(
        functools.partial(_ple_kernel, final=final),
        grid=(t // tm,),
        in_specs=[pl.BlockSpec((tm, d), lambda i: (i, 0)), pl.BlockSpec((tm, dp), lambda i: (i, 0)),
                  _full((1, d)), _full(gw_bf.shape), _full(pw_bf.shape), _full((1, d))],
        out_specs=pl.BlockSpec((tm, d), lambda i: (i, 0)),
        out_shape=jax.ShapeDtypeStruct((t, d), F32),
        compiler_params=_cparams("parallel"),
        name="ple",
    )(x, p, g, gw_bf, pw_bf, gf)


def _rwkv_prompt_scan(parts, batch, seq, n_heads):
    bh = batch * n_heads
    assert 2 * bh == LANES

    def rows(x):
        y = x.reshape(batch, seq, n_heads, HEAD).transpose(1, 3, 0, 2).reshape(seq, HEAD, bh)
        return jnp.concatenate([y, y], axis=-1)

    r, k, v, w, a, b = parts
    vt = v.reshape(batch, seq, n_heads, 2, HEAD // 2).transpose(1, 4, 3, 0, 2).reshape(seq, HEAD // 2, LANES)
    s0 = jnp.zeros((HEAD // 2, HEAD, LANES), F32)
    y, s_fin = _scan(rows(w), rows(a), rows(b), rows(k), rows(r), vt, s0)
    y = y.reshape(seq, HEAD // 2, 2, batch, n_heads).transpose(3, 0, 4, 2, 1).reshape(batch * seq, n_heads * HEAD)
    s_fin = s_fin.reshape(HEAD // 2, HEAD, 2, batch, n_heads).transpose(3, 4, 2, 0, 1)
    return y, s_fin.reshape(batch, n_heads, HEAD, HEAD)


def _rwkv_sample_scan(parts, wkv_prev, batch, n_heads):
    bh = batch * n_heads
    assert bh % LANES == 0
    rows = lambda x: x.reshape(batch, n_heads, HEAD).transpose(2, 0, 1).reshape(1, HEAD, bh)
    r, k, v, w, a, b = parts
    s0 = wkv_prev.transpose(2, 3, 0, 1).reshape(HEAD, HEAD, bh)
    y, s_fin = _scan(rows(w), rows(a), rows(b), rows(k), rows(r), rows(v), s0)
    y = y.reshape(HEAD, batch, n_heads).transpose(1, 2, 0).reshape(batch, n_heads * HEAD)
    return y, s_fin.reshape(HEAD, HEAD, batch, n_heads).transpose(2, 3, 0, 1)


def _layer(x, p, wkv_prev, shift_prev, lw, gf, final):
    batch, seq, d = x.shape
    t = batch * seq
    d_a = lw["w0"].shape[1]
    n_heads = d_a // HEAD
    p_a = lw["mu"].shape[1]
    xt = x.reshape(t, d)

    pa, pc = _proj(xt, lw["norm_mix_g"], lw["w_in"], p_a)
    pa3 = pa.reshape(batch, seq, p_a)
    prev = jnp.concatenate([shift_prev[:, None, :], pa3[:, :-1]], axis=1).reshape(t, p_a)
    r, k, v, w, a, b, g, bonus = _prep(pa, prev, lw["mu"], lw["w0"], lw["a0"], lw["wa2"], lw["g2"],
                                       lw["k_k"], lw["k_a"], lw["r_k"], lw["bd"])
    if wkv_prev is None:
        y, wkv_new = _rwkv_prompt_scan((r, k, v, w, a, b), batch, seq, n_heads)
        yc = _cmlp(pc, lw["cmlp_ln_g"], lw["cmlp_ln_b"], lw["cmlp_ws"], lw["cmlp_bs_rows"], lw["bd"])
        v_rows = None
    else:
        assert seq == 1
        y, wkv_new = _rwkv_sample_scan((r, k, v, w, a, b), wkv_prev, batch, n_heads)
        yc, v_rows = _cmlp_first(pc, lw["cmlp_ln_g"], lw["cmlp_ln_b"], lw["cmlp_w00"], lw["cmlp_b0"], lw["bd"])
        v_rows = v_rows.reshape(batch, seq, -1)
    x1 = _mix(xt, y, bonus, g, yc, lw["rwkv_ln_g"], lw["rwkv_ln_b"], lw["bd"], lw["w_out"])

    xn, st = _peerq(x1, lw["norm_ffn_g"], lw["peer_wq"], lw["peer_subkeys"])
    p1, p2, th = _peer_topk(st)
    x2 = _peer_dense(xn, x1, lw["peer_u"], lw["peer_vt"], p1, p2, th)

    x3 = _ple(x2, p.reshape(t, -1), lw["norm_ple_g"], lw["ple_gate_w"], lw["ple_w"], gf, final)
    return x3.reshape(batch, seq, d), wkv_new, pa3[:, -1], v_rows


def kernel(x_prompt, x_sample, state_wkv, state_shift, p_prompt, p_sample, norm_mix_g, w_in, shift_mu, rwkv_w0, rwkv_w2, rwkv_a0, rwkv_a2, rwkv_g2, rwkv_k_k, rwkv_k_a, rwkv_r_k, rwkv_ln_g, rwkv_ln_b, cmlp_ln_g, cmlp_ln_b, cmlp_ws, cmlp_bs, w_out, norm_ffn_g, peer_wq, peer_subkeys, peer_u, peer_v, norm_ple_g, ple_w, ple_gate_w, norm_final_g):
    depth = state_wkv.shape[0]
    batch = x_prompt.shape[0]
    d_a = rwkv_w0.shape[1]
    d_c = cmlp_ln_g.shape[1]
    p_a = shift_mu.shape[1]
    lora_w = rwkv_w2.shape[1]
    lora_a = rwkv_a2.shape[1]
    assert lora_w == HEAD and lora_a == HEAD and d_a == d_c
    gidx = jnp.arange(d_a) // HEAD
    bd = (gidx[:, None] == gidx[None, :]).astype(F32)
    row = lambda z: z.reshape(1, -1)
    gf = row(norm_final_g)

    hp, hs = x_prompt, x_sample
    outs = [[] for _ in range(5)]
    for i in range(depth):
        wa2 = jnp.zeros((lora_w + lora_a, 2 * d_a), F32)
        wa2 = wa2.at[:lora_w, :d_a].set(rwkv_w2[i]).at[lora_w:, d_a:].set(rwkv_a2[i])
        lw = dict(
            norm_mix_g=row(norm_mix_g[i]), w_in=w_in[i].astype(BF16), mu=row(shift_mu[i]),
            w0=row(rwkv_w0[i]), a0=row(rwkv_a0[i]), wa2=wa2.astype(BF16), g2=rwkv_g2[i].astype(BF16),
            k_k=row(rwkv_k_k[i]), k_a=row(rwkv_k_a[i]), r_k=row(rwkv_r_k[i]), bd=bd,
            rwkv_ln_g=row(rwkv_ln_g[i]), rwkv_ln_b=row(rwkv_ln_b[i]),
            cmlp_ln_g=row(cmlp_ln_g[i]), cmlp_ln_b=row(cmlp_ln_b[i]), cmlp_ws=cmlp_ws[i],
            cmlp_bs_rows=jnp.repeat(cmlp_bs[i].T, HEAD, axis=1),
            cmlp_w00=row(jnp.repeat(cmlp_ws[i][:, 0, 0], HEAD)), cmlp_b0=row(jnp.repeat(cmlp_bs[i][:, 0], HEAD)),
            w_out=w_out[i].astype(BF16), norm_ffn_g=row(norm_ffn_g[i]), peer_wq=peer_wq[i].astype(BF16),
            peer_subkeys=peer_subkeys[i].astype(BF16), peer_u=peer_u[i].astype(BF16),
            peer_vt=peer_v[i].astype(BF16).T, norm_ple_g=row(norm_ple_g[i]),
            ple_w=ple_w[i].astype(BF16), ple_gate_w=ple_gate_w[i].astype(BF16),
        )
        final = i == depth - 1
        shift0 = jnp.zeros((batch, p_a), F32)
        hp, wkv_p, shift_p, _ = _layer(hp, p_prompt[i], None, shift0, lw, gf, final)
        hs, wkv_s, shift_s, v_s = _layer(hs, p_sample[i], state_wkv[i], state_shift[i], lw, gf, final)
        for o, val in zip(outs, (wkv_p, shift_p, wkv_s, shift_s, v_s)):
            o.append(val)
    return (hp, hs) + tuple(jnp.stack(o) for o in outs)
```

```python
import functools
import math

import jax
import jax.numpy as jnp
from jax import lax
from jax.experimental import pallas as pl
from jax.experimental.pallas import tpu as pltpu

F32 = jnp.float32
BF16 = jnp.bfloat16
HIGHEST = lax.Precision.HIGHEST

LANES = 128
HEAD = 64
CHUNK = 128
N_KEYS = 128
PEER_TOPK = 16
RMS_EPS = 1e-6
LN_EPS = 1e-5
GN_EPS = 64e-5
SQRT_HALF = math.sqrt(0.5)
NEG_INF = float("-inf")
VMEM_LIMIT = 56 * 1024 * 1024


def _cparams(*sem):
    return pltpu.CompilerParams(dimension_semantics=sem, vmem_limit_bytes=VMEM_LIMIT)


def _rmsnorm(x, g):
    return x * lax.rsqrt(jnp.mean(x * x, axis=-1, keepdims=True) + RMS_EPS) * g


def _gelu(x):
    return 0.5 * x * (1.0 + lax.erf(x * SQRT_HALF))


def _group_sum(x, bd):
    return jnp.dot(x, bd, precision=HIGHEST, preferred_element_type=F32)


def _row_tile(n, want):
    t = min(n, want)
    assert n % t == 0, (n, t)
    return t


def _full(shape):
    return pl.BlockSpec(shape, lambda *_: (0,) * len(shape))


def _proj_kernel(x_ref, g_ref, w_ref, pa_ref, pc_ref, *, p_a):
    h = _rmsnorm(x_ref[...], g_ref[...])
    p = jnp.dot(h.astype(BF16), w_ref[...], preferred_element_type=F32)
    pa_ref[...] = p[:, :p_a]
    pc_ref[...] = p[:, p_a:]


def _proj(x, g, w_bf, p_a):
    t, d = x.shape
    p_tot = w_bf.shape[1]
    tm = _row_tile(t, 512)
    return pl.pallas_call(
        functools.partial(_proj_kernel, p_a=p_a),
        grid=(t // tm,),
        in_specs=[pl.BlockSpec((tm, d), lambda i: (i, 0)), _full((1, d)), _full((d, p_tot))],
        out_specs=[pl.BlockSpec((tm, p_a), lambda i: (i, 0)),
                   pl.BlockSpec((tm, p_tot - p_a), lambda i: (i, 0))],
        out_shape=[jax.ShapeDtypeStruct((t, p_a), F32), jax.ShapeDtypeStruct((t, p_tot - p_a), F32)],
        compiler_params=_cparams("parallel"),
        name="proj",
    )(x, g, w_bf)


def _prep_kernel(pa_ref, prev_ref, mu_ref, w0_ref, a0_ref, wa2_ref, g2_ref, kk_ref, ka_ref, rk_ref, bd_ref,
                 r_o, k_o, v_o, w_o, a_o, b_o, g_o, bonus_o, *, d_a):
    pa = pa_ref[...]
    xm = pa + (prev_ref[...] - pa) * mu_ref[...]
    r = xm[:, 0:d_a]
    k = xm[:, d_a:2 * d_a]
    v = xm[:, 2 * d_a:3 * d_a]
    wal = xm[:, 3 * d_a:3 * d_a + LANES]
    gl = xm[:, 3 * d_a + LANES:]
    lane = lax.broadcasted_iota(jnp.int32, wal.shape, 1)
    wal = jnp.where(lane < HEAD, jnp.tanh(wal), wal)
    lo = jnp.dot(wal.astype(BF16), wa2_ref[...], preferred_element_type=F32)
    z = -(w0_ref[...] + lo[:, :d_a])
    softplus = jnp.maximum(z, 0.0) + jnp.log1p(jnp.exp(-jnp.abs(z)))
    decay = jnp.exp(-jnp.exp(-softplus - 0.5))
    a = jax.nn.sigmoid(a0_ref[...] + lo[:, d_a:])
    g = jnp.dot(jax.nn.sigmoid(gl).astype(BF16), g2_ref[...], preferred_element_type=F32)
    bd = bd_ref[...]
    kk = k * kk_ref[...]
    kk = kk / jnp.maximum(jnp.sqrt(_group_sum(kk * kk, bd)), 1e-12)
    kh = k * (1.0 + (a - 1.0) * ka_ref[...])
    r_o[...] = r
    k_o[...] = kh
    v_o[...] = v
    w_o[...] = decay
    a_o[...] = -kk
    b_o[...] = kk * a
    g_o[...] = g
    bonus_o[...] = _group_sum(r * kh * rk_ref[...], bd) * v


def _prep(pa, prev, mu, w0, a0, wa2_bf, g2_bf, k_k, k_a, r_k, bd):
    t, p_a = pa.shape
    d_a = w0.shape[1]
    tm = _row_tile(t, 256)
    row = lambda n: pl.BlockSpec((tm, n), lambda i: (i, 0))
    return pl.pallas_call(
        functools.partial(_prep_kernel, d_a=d_a),
        grid=(t // tm,),
        in_specs=[row(p_a), row(p_a), _full((1, p_a)), _full((1, d_a)), _full((1, d_a)),
                  _full(wa2_bf.shape), _full(g2_bf.shape), _full((1, d_a)), _full((1, d_a)),
                  _full((1, d_a)), _full(bd.shape)],
        out_specs=[row(d_a)] * 8,
        out_shape=[jax.ShapeDtypeStruct((t, d_a), F32)] * 8,
        compiler_params=_cparams("parallel"),
        name="rwkv_prep",
    )(pa, prev, mu, w0, a0, wa2_bf, g2_bf, k_k, k_a, r_k, bd)


def _recurrence(st_ref, w_ref, a_ref, b_ref, k_ref, r_ref, v_ref, y_ref, tl, nv):
    def step(t, carry):
        for i in range(nv):
            s = st_ref[i]
            sa = jnp.sum(s * a_ref[t], axis=0, keepdims=True)
            s = s * w_ref[t] + sa * b_ref[t] + v_ref[t, pl.ds(i, 1), :] * k_ref[t]
            st_ref[i] = s
            y_ref[t, pl.ds(i, 1), :] = jnp.sum(s * r_ref[t], axis=0, keepdims=True)
        return carry

    lax.fori_loop(0, tl, step, 0)


def _scan_kernel(w_ref, a_ref, b_ref, k_ref, r_ref, v_ref, s0_ref, y_ref, sfin_ref, st_ref, *, tl, nv):
    l = pl.program_id(1)

    @pl.when(l == 0)
    def _():
        st_ref[...] = s0_ref[...]

    _recurrence(st_ref, w_ref, a_ref, b_ref, k_ref, r_ref, v_ref, y_ref, tl, nv)

    @pl.when(l == pl.num_programs(1) - 1)
    def _():
        sfin_ref[...] = st_ref[...]


def _scan(w, a, b, k, r, v, s0):
    l, n, gl = w.shape
    nv = v.shape[1]
    tl = _row_tile(l, 64)
    rows = pl.BlockSpec((tl, n, LANES), lambda g, i: (i, 0, g))
    vals = pl.BlockSpec((tl, nv, LANES), lambda g, i: (i, 0, g))
    state = pl.BlockSpec((nv, n, LANES), lambda g, i: (0, 0, g))
    return pl.pallas_call(
        functools.partial(_scan_kernel, tl=tl, nv=nv),
        grid=(gl // LANES, l // tl),
        in_specs=[rows] * 5 + [vals, state],
        out_specs=[vals, state],
        out_shape=[jax.ShapeDtypeStruct((l, nv, gl), F32), jax.ShapeDtypeStruct((nv, n, gl), F32)],
        scratch_shapes=[pltpu.VMEM((nv, n, LANES), F32)],
        compiler_params=_cparams("parallel", "arbitrary"),
        name="rwkv_scan",
    )(w, a, b, k, r, v, s0)


def _keys_to_lanes(x, n_heads):
    pieces = [x[:, h * HEAD:(h + 1) * HEAD] for h in range(n_heads)]
    return jnp.concatenate(pieces + pieces, axis=0).T


def _values_to_lanes(x, n_heads):
    half = HEAD // 2
    lo = [x[:, h * HEAD:h * HEAD + half] for h in range(n_heads)]
    hi = [x[:, h * HEAD + half:(h + 1) * HEAD] for h in range(n_heads)]
    return jnp.concatenate(lo + hi, axis=0).T


def _values_from_lanes(y, n_heads, batch):
    yt = y.T
    hb = n_heads * batch
    full = jnp.concatenate([yt[0:hb], yt[hb:2 * hb]], axis=1)
    return jnp.concatenate([full[h * batch:(h + 1) * batch] for h in range(n_heads)], axis=1)


N_KEY_ARRAYS = 5


def _scan_tokens_kernel(*refs, tl, nv, n_heads, batch):
    key_refs, v_ref = refs[:N_KEY_ARRAYS], refs[N_KEY_ARRAYS]
    y_ref, sfin_ref, st_ref = refs[N_KEY_ARRAYS + 1:N_KEY_ARRAYS + 4]
    slot_refs = refs[N_KEY_ARRAYS + 4:]
    n_slot = N_KEY_ARRAYS + 2
    slots = (slot_refs[:n_slot], slot_refs[n_slot:])
    l = pl.program_id(0)

    @pl.when(l == 0)
    def _():
        st_ref[...] = jnp.zeros_like(st_ref)
        slots[1][-1][...] = jnp.zeros_like(slots[1][-1])

    def fill(slot, t):
        for src, dst in zip(key_refs, slot[:N_KEY_ARRAYS]):
            dst[...] = _keys_to_lanes(src[:, t, :], n_heads)
        slot[N_KEY_ARRAYS][...] = _values_to_lanes(v_ref[:, t, :], n_heads)

    fill(slots[0], 0)
    spread = nv // (n_slot + 1)

    def one_step(cur, other, t_next, t_prev):
        wt, at, bt, kt, rt, vt, yt = cur
        for i in range(nv):
            s = st_ref[i]
            sa = jnp.sum(s * at[...], axis=0, keepdims=True)
            s = s * wt[...] + sa * bt[...] + vt[pl.ds(i, 1), :] * kt[...]
            st_ref[i] = s
            yt[pl.ds(i, 1), :] = jnp.sum(s * rt[...], axis=0, keepdims=True)
            if i % spread == 0:
                job = i // spread
                if job == 0:
                    y_ref[:, t_prev, :] = _values_from_lanes(other[-1][...], n_heads, batch)
                elif job <= N_KEY_ARRAYS:
                    other[job - 1][...] = _keys_to_lanes(key_refs[job - 1][:, t_next, :], n_heads)
                elif job == N_KEY_ARRAYS + 1:
                    other[N_KEY_ARRAYS][...] = _values_to_lanes(v_ref[:, t_next, :], n_heads)

    def two_steps(j, carry):
        t = 2 * j
        one_step(slots[0], slots[1], t + 1, jnp.maximum(t - 1, 0))
        one_step(slots[1], slots[0], jnp.minimum(t + 2, tl - 1), t)
        return carry

    lax.fori_loop(0, tl // 2, two_steps, 0)
    y_ref[:, tl - 1, :] = _values_from_lanes(slots[1][-1][...], n_heads, batch)

    @pl.when(l == pl.num_programs(0) - 1)
    def _():
        sfin_ref[...] = st_ref[...]


def _scan_tokens(w, a, b, k, r, v):
    batch, l, d = w.shape
    n_heads = d // HEAD
    assert 2 * batch * n_heads == LANES
    nv = HEAD // 2
    tl = _row_tile(l, 64)
    tok = pl.BlockSpec((batch, tl, d), lambda i: (0, i, 0))
    state = pl.BlockSpec((nv, HEAD, LANES), lambda i: (0, 0, 0))
    assert tl % 2 == 0
    slot = [pltpu.VMEM((HEAD, LANES), F32)] * N_KEY_ARRAYS + [pltpu.VMEM((nv, LANES), F32)] * 2
    return pl.pallas_call(
        functools.partial(_scan_tokens_kernel, tl=tl, nv=nv, n_heads=n_heads, batch=batch),
        grid=(l // tl,),
        in_specs=[tok] * 6,
        out_specs=[tok, state],
        out_shape=[jax.ShapeDtypeStruct((batch, l, d), F32), jax.ShapeDtypeStruct((nv, HEAD, LANES), F32)],
        scratch_shapes=[pltpu.VMEM((nv, HEAD, LANES), F32)] + slot + slot,
        compiler_params=_cparams("arbitrary"),
        name="rwkv_scan_tokens",
    )(w, a, b, k, r, v)


def _cmlp_norm(pc, lng, lnb, bd, d_c):
    ge = _gelu(pc)
    u = ge[:, :d_c]
    v = ge[:, d_c:]
    mean = _group_sum(v, bd) * (1.0 / HEAD)
    d = v - mean
    var = _group_sum(d * d, bd) * (1.0 / HEAD)
    return u, d * lax.rsqrt(var + LN_EPS) * lng + lnb


def _cmlp_kernel(pc_ref, lng_ref, lnb_ref, ws_ref, bs_ref, bd_ref, yc_ref, *, d_c, n_chunks):
    u, vn = _cmlp_norm(pc_ref[...], lng_ref[...], lnb_ref[...], bd_ref[...], d_c)
    vb = vn.astype(BF16)
    ri = lax.broadcasted_iota(jnp.int32, (CHUNK, CHUNK), 0)
    ci = lax.broadcasted_iota(jnp.int32, (CHUNK, CHUNK), 1)
    lane = lax.broadcasted_iota(jnp.int32, (CHUNK, LANES), 1)
    n_groups = d_c // HEAD
    wsm = [jnp.where(ri >= ci, ws_ref[g], 0.0).astype(BF16) for g in range(n_groups)]
    for c in range(n_chunks):
        rows = slice(c * CHUNK, (c + 1) * CHUNK)
        for p in range(n_groups // 2):
            cols = slice(p * LANES, (p + 1) * LANES)
            vp = vb[rows, cols]
            s0 = jnp.dot(wsm[2 * p], vp, preferred_element_type=F32)
            s1 = jnp.dot(wsm[2 * p + 1], vp, preferred_element_type=F32)
            s = jnp.where(lane < HEAD, s0, s1) + bs_ref[:, cols]
            yc_ref[rows, cols] = u[rows, cols] * s


def _cmlp(pc, lng, lnb, ws, bs_rows, bd):
    t, p_c = pc.shape
    d_c = p_c // 2
    tm = _row_tile(t, 256)
    assert tm % CHUNK == 0
    return pl.pallas_call(
        functools.partial(_cmlp_kernel, d_c=d_c, n_chunks=tm // CHUNK),
        grid=(t // tm,),
        in_specs=[pl.BlockSpec((tm, p_c), lambda i: (i, 0)), _full((1, d_c)), _full((1, d_c)),
                  _full(ws.shape), _full(bs_rows.shape), _full(bd.shape)],
        out_specs=pl.BlockSpec((tm, d_c), lambda i: (i, 0)),
        out_shape=jax.ShapeDtypeStruct((t, d_c), F32),
        compiler_params=_cparams("parallel"),
        name="cmlp",
    )(pc, lng, lnb, ws, bs_rows, bd)


def _cmlp_first_kernel(pc_ref, lng_ref, lnb_ref, w00_ref, b0_ref, bd_ref, yc_ref, vn_ref, *, d_c):
    u, vn = _cmlp_norm(pc_ref[...], lng_ref[...], lnb_ref[...], bd_ref[...], d_c)
    yc_ref[...] = u * (w00_ref[...] * vn + b0_ref[...])
    vn_ref[...] = vn


def _cmlp_first(pc, lng, lnb, w00, b0, bd):
    t, p_c = pc.shape
    d_c = p_c // 2
    return pl.pallas_call(
        functools.partial(_cmlp_first_kernel, d_c=d_c),
        grid=(1,),
        in_specs=[_full(pc.shape), _full((1, d_c)), _full((1, d_c)), _full((1, d_c)), _full((1, d_c)),
                  _full(bd.shape)],
        out_specs=[_full((t, d_c))] * 2,
        out_shape=[jax.ShapeDtypeStruct((t, d_c), F32)] * 2,
        compiler_params=_cparams("arbitrary"),
        name="cmlp_first",
    )(pc, lng, lnb, w00, b0, bd)


def _mix_kernel(x_ref, y_ref, bonus_ref, g_ref, yc_ref, lng_ref, lnb_ref, bd_ref, wo_ref, o_ref, *, d_a):
    bd = bd_ref[...]
    y = y_ref[...]
    mean = _group_sum(y, bd) * (1.0 / HEAD)
    d = y - mean
    var = _group_sum(d * d, bd) * (1.0 / HEAD)
    yn = d * lax.rsqrt(var + GN_EPS) * lng_ref[...] + lnb_ref[...]
    ya = (yn + bonus_ref[...]) * g_ref[...]
    o_ref[...] = (x_ref[...]
                  + jnp.dot(ya.astype(BF16), wo_ref[0:d_a, :], preferred_element_type=F32)
                  + jnp.dot(yc_ref[...].astype(BF16), wo_ref[d_a:, :], preferred_element_type=F32))


def _mix(x, y, bonus, g, yc, lng, lnb, bd, wo_bf):
    t, d = x.shape
    d_a = y.shape[1]
    d_c = yc.shape[1]
    tm = _row_tile(t, 256)
    row = lambda n: pl.BlockSpec((tm, n), lambda i: (i, 0))
    return pl.pallas_call(
        functools.partial(_mix_kernel, d_a=d_a),
        grid=(t // tm,),
        in_specs=[row(d), row(d_a), row(d_a), row(d_a), row(d_c), _full((1, d_a)), _full((1, d_a)),
                  _full(bd.shape), _full(wo_bf.shape)],
        out_specs=row(d),
        out_shape=jax.ShapeDtypeStruct((t, d), F32),
        compiler_params=_cparams("parallel"),
        name="mix_out",
    )(x, y, bonus, g, yc, lng, lnb, bd, wo_bf)


def _peerq_kernel(x_ref, g_ref, wq_ref, sk_ref, xn_ref, st_ref, *, n_hc):
    hb = _rmsnorm(x_ref[...], g_ref[...]).astype(BF16)
    xn_ref[...] = hb
    qb = jnp.dot(hb, wq_ref[...], preferred_element_type=F32).astype(BF16)
    for hc in range(n_hc):
        st_ref[hc] = lax.dot_general(sk_ref[hc % 2], qb[:, hc * LANES:(hc + 1) * LANES],
                                     (((1,), (1,)), ((), ())), preferred_element_type=F32)


def _peerq(x, g, wq_bf, sk_bf):
    t, d = x.shape
    n_hc = wq_bf.shape[1] // LANES
    tm = _row_tile(t, 512)
    return pl.pallas_call(
        functools.partial(_peerq_kernel, n_hc=n_hc),
        grid=(t // tm,),
        in_specs=[pl.BlockSpec((tm, d), lambda i: (i, 0)), _full((1, d)), _full(wq_bf.shape),
                  _full(sk_bf.shape)],
        out_specs=[pl.BlockSpec((tm, d), lambda i: (i, 0)),
                   pl.BlockSpec((n_hc, N_KEYS, tm), lambda i: (0, 0, i))],
        out_shape=[jax.ShapeDtypeStruct((t, d), BF16), jax.ShapeDtypeStruct((n_hc, N_KEYS, t), F32)],
        compiler_params=_cparams("parallel"),
        name="peer_query",
    )(x, g, wq_bf, sk_bf)


def _batcher_network(n):
    pairs = []

    def merge(lo, m, r):
        step = 2 * r
        if step < m:
            merge(lo, m, step)
            merge(lo + r, m, step)
            pairs.extend((i, i + r) for i in range(lo + r, lo + m - r, step))
        else:
            pairs.append((lo, lo + r))

    def sort(lo, m):
        if m > 1:
            sort(lo, m // 2)
            sort(lo + m // 2, m // 2)
            merge(lo, m, 1)

    sort(0, n)
    return pairs


SUBLANES = 8
KEY_SLABS = N_KEYS // SUBLANES
SLAB_NETWORK = _batcher_network(KEY_SLABS)


def _pop_heads(lists, extra_heads, n):
    lists, extra_heads = list(lists), list(extra_heads)
    tops = []
    for r in range(n):
        head = lists[0]
        for e in extra_heads:
            head = jnp.maximum(head, e)
        m = jnp.max(head, axis=0, keepdims=True)
        tops.append(m)
        hit = lists[0] == m
        for d in range(min(n - 1 - r, len(lists))):
            nxt = lists[d + 1] if d + 1 < len(lists) else NEG_INF
            lists[d] = jnp.where(hit, nxt, lists[d])
        if r < n - 1:
            extra_heads = [jnp.where(e == m, NEG_INF, e) for e in extra_heads]
    return tops


def _top_sorted(s, n):
    slabs = [s[SUBLANES * r:SUBLANES * (r + 1)] for r in range(KEY_SLABS)]
    for i, j in SLAB_NETWORK:
        slabs[i], slabs[j] = jnp.maximum(slabs[i], slabs[j]), jnp.minimum(slabs[i], slabs[j])
    return _pop_heads(slabs, [], n)


def _peer_topk_kernel(st_ref, p1_ref, p2_ref, th_ref, *, n_heads):
    k = PEER_TOPK
    n = k + 1
    n_pad = -(-n // SUBLANES) * SUBLANES

    def head(h, carry):
        s1 = st_ref[2 * h]
        s2 = st_ref[2 * h + 1]
        t1 = _top_sorted(s1, n)
        t2 = _top_sorted(s2, n)
        row = lax.broadcasted_iota(jnp.int32, (n_pad, s1.shape[1]), 0)
        v1 = jnp.full((n_pad, s1.shape[1]), NEG_INF, F32)
        for i in range(n):
            v1 = jnp.where(row == i, t1[i], v1)
        row8 = row[0:SUBLANES]
        lists = [v1[0:SUBLANES] + t2[0]]
        for j in range(1, n):
            lists.append(jnp.where(row8 < n // (j + 1), v1[0:SUBLANES] + t2[j], NEG_INF))
        singles = [v1[SUBLANES * b:SUBLANES * (b + 1)] + t2[0] for b in range(1, n_pad // SUBLANES)]
        c = _pop_heads(lists, singles, n)
        m = c[0]
        z = jnp.ones_like(m)
        for r in range(1, k):
            z = z + jnp.exp(c[r] - m)
        inv_z = 1.0 / z
        th_ref[pl.ds(h, 1), :] = jnp.exp(0.5 * (c[k - 1] + c[k]) - m) * inv_z
        p1_ref[h] = jnp.exp(s1 - t1[0]) * inv_z
        p2_ref[h] = jnp.exp(s2 - t2[0])
        return carry

    lax.fori_loop(0, n_heads, head, 0)


def _peer_topk(st):
    n_hc, n, t = st.shape
    n_heads = n_hc // 2
    tk = LANES
    fac = pl.BlockSpec((n_heads, n, tk), lambda i: (0, 0, i))
    return pl.pallas_call(
        functools.partial(_peer_topk_kernel, n_heads=n_heads),
        grid=(t // tk,),
        in_specs=[pl.BlockSpec((n_hc, n, tk), lambda i: (0, 0, i))],
        out_specs=[fac, fac, pl.BlockSpec((n_heads, tk), lambda i: (0, i))],
        out_shape=[jax.ShapeDtypeStruct((n_heads, n, t), F32)] * 2 + [jax.ShapeDtypeStruct((n_heads, t), F32)],
        compiler_params=_cparams("parallel"),
        name="peer_topk",
    )(st)


ROW_BLK = 32
KEYS_PER_TILE = 8


def _peer_dense_kernel(xn_ref, x_ref, u_ref, vt_ref, p1_ref, p2_ref, th_ref, o_ref, ht_ref, at_ref, acc_ref,
                       *, n_heads, tm):
    j = pl.program_id(1)

    @pl.when(j == 0)
    def _():
        acc_ref[...] = jnp.zeros_like(acc_ref)

    ht_ref[...] = lax.dot_general(u_ref[...], xn_ref[...], (((1,), (1,)), ((), ())),
                                  preferred_element_type=F32)

    for k in range(KEYS_PER_TILE):
        for lc in range(tm // LANES):
            cols = pl.ds(lc * LANES, LANES)
            p1_rows = [p1_ref[h, k:k + 1, cols] for h in range(n_heads)]
            th_rows = [th_ref[h:h + 1, cols] for h in range(n_heads)]
            for rb in range(N_KEYS // ROW_BLK):
                r2 = pl.ds(rb * ROW_BLK, ROW_BLK)
                r0 = pl.ds(k * N_KEYS + rb * ROW_BLK, ROW_BLK)
                gsum = jnp.zeros((ROW_BLK, LANES), F32)
                for h in range(n_heads):
                    w = p1_rows[h] * p2_ref[h, r2, cols]
                    gsum = gsum + jnp.where(w >= th_rows[h], w, 0.0)
                at_ref[r0, cols] = (_gelu(ht_ref[r0, cols]) * gsum).astype(BF16)

    acc_ref[...] += jnp.dot(vt_ref[...], at_ref[...], preferred_element_type=F32)

    @pl.when(j == pl.num_programs(1) - 1)
    def _():
        o_ref[...] = x_ref[...] + acc_ref[...].T


def _peer_dense(xn, x, u_bf, vt_bf, p1, p2, th):
    t, d = x.shape
    n_exp = u_bf.shape[0]
    n_heads = p1.shape[0]
    tm = _row_tile(t, 512)
    te = KEYS_PER_TILE * N_KEYS
    assert n_exp == N_KEYS * N_KEYS
    return pl.pallas_call(
        functools.partial(_peer_dense_kernel, n_heads=n_heads, tm=tm),
        grid=(t // tm, n_exp // te),
        in_specs=[pl.BlockSpec((tm, d), lambda i, j: (i, 0)),
                  pl.BlockSpec((tm, d), lambda i, j: (i, 0)),
                  pl.BlockSpec((te, d), lambda i, j: (j, 0)),
                  pl.BlockSpec((d, te), lambda i, j: (0, j)),
                  pl.BlockSpec((n_heads, KEYS_PER_TILE, tm), lambda i, j: (0, j, i)),
                  pl.BlockSpec((n_heads, N_KEYS, tm), lambda i, j: (0, 0, i)),
                  pl.BlockSpec((n_heads, tm), lambda i, j: (0, i))],
        out_specs=pl.BlockSpec((tm, d), lambda i, j: (i, 0)),
        out_shape=jax.ShapeDtypeStruct((t, d), F32),
        scratch_shapes=[pltpu.VMEM((te, tm), F32), pltpu.VMEM((te, tm), BF16), pltpu.VMEM((d, tm), F32)],
        compiler_params=_cparams("parallel", "arbitrary"),
        name="peer_dense",
    )(xn, x, u_bf, vt_bf, p1, p2, th)


def _ple_kernel(x_ref, p_ref, g_ref, gw_ref, pw_ref, gf_ref, o_ref, *, final):
    x = x_ref[...]
    hn = _rmsnorm(x, g_ref[...])
    gate = jax.nn.sigmoid(jnp.dot(hn.astype(BF16), gw_ref[...], preferred_element_type=F32))
    e = jnp.dot(p_ref[...].astype(BF16), pw_ref[...], preferred_element_type=F32)
    x = x + e * gate
    o_ref[...] = _rmsnorm(x, gf_ref[...]) if final else x


def _ple(x, p, g, gw_bf, pw_bf, gf, final):
    t, d = x.shape
    dp = p.shape[1]
    tm = _row_tile(t, 512)
    return pl.pallas_call(
        functools.partial(_ple_kernel, final=final),
        grid=(t // tm,),
        in_specs=[pl.BlockSpec((tm, d), lambda i: (i, 0)), pl.BlockSpec((tm, dp), lambda i: (i, 0)),
                  _full((1, d)), _full(gw_bf.shape), _full(pw_bf.shape), _full((1, d))],
        out_specs=pl.BlockSpec((tm, d), lambda i: (i, 0)),
        out_shape=jax.ShapeDtypeStruct((t, d), F32),
        compiler_params=_cparams("parallel"),
        name="ple",
    )(x, p, g, gw_bf, pw_bf, gf)


def _rwkv_prompt_scan(parts, batch, seq, n_heads):
    r, k, v, w, a, b = (x.reshape(batch, seq, n_heads * HEAD) for x in parts)
    y, s_fin = _scan_tokens(w, a, b, k, r, v)
    s_fin = s_fin.reshape(HEAD // 2, HEAD, 2, n_heads, batch).transpose(4, 3, 2, 0, 1)
    return y.reshape(batch * seq, n_heads * HEAD), s_fin.reshape(batch, n_heads, HEAD, HEAD)


def _rwkv_sample_scan(parts, wkv_prev, batch, n_heads):
    bh = batch * n_heads
    assert bh % LANES == 0
    rows = lambda x: x.reshape(batch, n_heads, HEAD).transpose(2, 0, 1).reshape(1, HEAD, bh)
    r, k, v, w, a, b = parts
    s0 = wkv_prev.transpose(2, 3, 0, 1).reshape(HEAD, HEAD, bh)
    y, s_fin = _scan(rows(w), rows(a), rows(b), rows(k), rows(r), rows(v), s0)
    y = y.reshape(HEAD, batch, n_heads).transpose(1, 2, 0).reshape(batch, n_heads * HEAD)
    return y, s_fin.reshape(HEAD, HEAD, batch, n_heads).transpose(2, 3, 0, 1)


def _layer(x, p, wkv_prev, shift_prev, lw, gf, final):
    batch, seq, d = x.shape
    t = batch * seq
    d_a = lw["w0"].shape[1]
    n_heads = d_a // HEAD
    p_a = lw["mu"].shape[1]
    xt = x.reshape(t, d)

    pa, pc = _proj(xt, lw["norm_mix_g"], lw["w_in"], p_a)
    pa3 = pa.reshape(batch, seq, p_a)
    prev = jnp.concatenate([shift_prev[:, None, :], pa3[:, :-1]], axis=1).reshape(t, p_a)
    r, k, v, w, a, b, g, bonus = _prep(pa, prev, lw["mu"], lw["w0"], lw["a0"], lw["wa2"], lw["g2"],
                                       lw["k_k"], lw["k_a"], lw["r_k"], lw["bd"])
    if wkv_prev is None:
        y, wkv_new = _rwkv_prompt_scan((r, k, v, w, a, b), batch, seq, n_heads)
        yc = _cmlp(pc, lw["cmlp_ln_g"], lw["cmlp_ln_b"], lw["cmlp_ws"], lw["cmlp_bs_rows"], lw["bd"])
        v_rows = None
    else:
        assert seq == 1
        y, wkv_new = _rwkv_sample_scan((r, k, v, w, a, b), wkv_prev, batch, n_heads)
        yc, v_rows = _cmlp_first(pc, lw["cmlp_ln_g"], lw["cmlp_ln_b"], lw["cmlp_w00"], lw["cmlp_b0"], lw["bd"])
        v_rows = v_rows.reshape(batch, seq, -1)
    x1 = _mix(xt, y, bonus, g, yc, lw["rwkv_ln_g"], lw["rwkv_ln_b"], lw["bd"], lw["w_out"])

    xn, st = _peerq(x1, lw["norm_ffn_g"], lw["peer_wq"], lw["peer_subkeys"])
    p1, p2, th = _peer_topk(st)
    x2 = _peer_dense(xn, x1, lw["peer_u"], lw["peer_vt"], p1, p2, th)

    x3 = _ple(x2, p.reshape(t, -1), lw["norm_ple_g"], lw["ple_gate_w"], lw["ple_w"], gf, final)
    return x3.reshape(batch, seq, d), wkv_new, pa3[:, -1], v_rows


def kernel(x_prompt, x_sample, state_wkv, state_shift, p_prompt, p_sample, norm_mix_g, w_in, shift_mu, rwkv_w0, rwkv_w2, rwkv_a0, rwkv_a2, rwkv_g2, rwkv_k_k, rwkv_k_a, rwkv_r_k, rwkv_ln_g, rwkv_ln_b, cmlp_ln_g, cmlp_ln_b, cmlp_ws, cmlp_bs, w_out, norm_ffn_g, peer_wq, peer_subkeys, peer_u, peer_v, norm_ple_g, ple_w, ple_gate_w, norm_final_g):
    depth = state_wkv.shape[0]
    batch = x_prompt.shape[0]
    d_a = rwkv_w0.shape[1]
    d_c = cmlp_ln_g.shape[1]
    p_a = shift_mu.shape[1]
    lora_w = rwkv_w2.shape[1]
    lora_a = rwkv_a2.shape[1]
    assert lora_w == HEAD and lora_a == HEAD and d_a == d_c
    gidx = jnp.arange(d_a) // HEAD
    bd = (gidx[:, None] == gidx[None, :]).astype(F32)
    row = lambda z: z.reshape(1, -1)
    gf = row(norm_final_g)

    hp, hs = x_prompt, x_sample
    outs = [[] for _ in range(5)]
    for i in range(depth):
        wa2 = jnp.zeros((lora_w + lora_a, 2 * d_a), F32)
        wa2 = wa2.at[:lora_w, :d_a].set(rwkv_w2[i]).at[lora_w:, d_a:].set(rwkv_a2[i])
        lw = dict(
            norm_mix_g=row(norm_mix_g[i]), w_in=w_in[i].astype(BF16), mu=row(shift_mu[i]),
            w0=row(rwkv_w0[i]), a0=row(rwkv_a0[i]), wa2=wa2.astype(BF16), g2=rwkv_g2[i].astype(BF16),
            k_k=row(rwkv_k_k[i]), k_a=row(rwkv_k_a[i]), r_k=row(rwkv_r_k[i]), bd=bd,
            rwkv_ln_g=row(rwkv_ln_g[i]), rwkv_ln_b=row(rwkv_ln_b[i]),
            cmlp_ln_g=row(cmlp_ln_g[i]), cmlp_ln_b=row(cmlp_ln_b[i]), cmlp_ws=cmlp_ws[i],
            cmlp_bs_rows=jnp.repeat(cmlp_bs[i].T, HEAD, axis=1),
            cmlp_w00=row(jnp.repeat(cmlp_ws[i][:, 0, 0], HEAD)), cmlp_b0=row(jnp.repeat(cmlp_bs[i][:, 0], HEAD)),
            w_out=w_out[i].astype(BF16), norm_ffn_g=row(norm_ffn_g[i]), peer_wq=peer_wq[i].astype(BF16),
            peer_subkeys=peer_subkeys[i].astype(BF16), peer_u=peer_u[i].astype(BF16),
            peer_vt=peer_v[i].astype(BF16).T, norm_ple_g=row(norm_ple_g[i]),
            ple_w=ple_w[i].astype(BF16), ple_gate_w=ple_gate_w[i].astype(BF16),
        )
        final = i == depth - 1
        shift0 = jnp.zeros((batch, p_a), F32)
        hp, wkv_p, shift_p, _ = _layer(hp, p_prompt[i], None, shift0, lw, gf, final)
        hs, wkv_s, shift_s, v_s = _layer(hs, p_sample[i], state_wkv[i], state_shift[i], lw, gf, final)
        for o, val in zip(outs, (wkv_p, shift_p, wkv_s, shift_s, v_s)):
            o.append(val)
    return (hp, hs) + tuple(jnp.stack(o) for o in outs)
```

```python
import functools
import math

import jax
import jax.numpy as jnp
from jax import lax
from jax.experimental import pallas as pl
from jax.experimental.pallas import tpu as pltpu

F32 = jnp.float32
BF16 = jnp.bfloat16
HIGHEST = lax.Precision.HIGHEST

LANES = 128
HEAD = 64
CHUNK = 128
N_KEYS = 128
PEER_TOPK = 16
RMS_EPS = 1e-6
LN_EPS = 1e-5
GN_EPS = 64e-5
SQRT_HALF = math.sqrt(0.5)
NEG_INF = float("-inf")
VMEM_LIMIT = 56 * 1024 * 1024


def _cparams(*sem):
    return pltpu.CompilerParams(dimension_semantics=sem, vmem_limit_bytes=VMEM_LIMIT)


def _rmsnorm(x, g):
    return x * lax.rsqrt(jnp.mean(x * x, axis=-1, keepdims=True) + RMS_EPS) * g


def _gelu(x):
    return 0.5 * x * (1.0 + lax.erf(x * SQRT_HALF))


def _group_sum(x, bd):
    return jnp.dot(x, bd, precision=HIGHEST, preferred_element_type=F32)


def _row_tile(n, want):
    t = min(n, want)
    assert n % t == 0, (n, t)
    return t


def _full(shape):
    return pl.BlockSpec(shape, lambda *_: (0,) * len(shape))


def _proj_kernel(x_ref, g_ref, w_ref, pa_ref, pc_ref, *, p_a):
    h = _rmsnorm(x_ref[...], g_ref[...])
    p = jnp.dot(h.astype(BF16), w_ref[...], preferred_element_type=F32)
    pa_ref[...] = p[:, :p_a]
    pc_ref[...] = p[:, p_a:]


def _proj(x, g, w_bf, p_a):
    t, d = x.shape
    p_tot = w_bf.shape[1]
    tm = _row_tile(t, 512)
    return pl.pallas_call(
        functools.partial(_proj_kernel, p_a=p_a),
        grid=(t // tm,),
        in_specs=[pl.BlockSpec((tm, d), lambda i: (i, 0)), _full((1, d)), _full((d, p_tot))],
        out_specs=[pl.BlockSpec((tm, p_a), lambda i: (i, 0)),
                   pl.BlockSpec((tm, p_tot - p_a), lambda i: (i, 0))],
        out_shape=[jax.ShapeDtypeStruct((t, p_a), F32), jax.ShapeDtypeStruct((t, p_tot - p_a), F32)],
        compiler_params=_cparams("parallel"),
        name="proj",
    )(x, g, w_bf)


def _prep_kernel(pa_ref, prev_ref, mu_ref, w0_ref, a0_ref, wa2_ref, g2_ref, kk_ref, ka_ref, rk_ref, bd_ref,
                 r_o, k_o, v_o, w_o, a_o, b_o, g_o, bonus_o, *carry, d_a):
    pa = pa_ref[...]
    if carry:
        (carry_ref,) = carry
        first = jnp.where(pl.program_id(1) == 0, prev_ref[0], carry_ref[...])
        row = lax.broadcasted_iota(jnp.int32, pa.shape, 0)
        prev = jnp.where(row == 0, first, pltpu.roll(pa, 1, 0))
        carry_ref[...] = pa[pa.shape[0] - 1:, :]
    else:
        prev = prev_ref[...]
    xm = pa + (prev - pa) * mu_ref[...]
    r = xm[:, 0:d_a]
    k = xm[:, d_a:2 * d_a]
    v = xm[:, 2 * d_a:3 * d_a]
    wal = xm[:, 3 * d_a:3 * d_a + LANES]
    gl = xm[:, 3 * d_a + LANES:]
    lane = lax.broadcasted_iota(jnp.int32, wal.shape, 1)
    wal = jnp.where(lane < HEAD, jnp.tanh(wal), wal)
    lo = jnp.dot(wal.astype(BF16), wa2_ref[...], preferred_element_type=F32)
    z = -(w0_ref[...] + lo[:, :d_a])
    softplus = jnp.maximum(z, 0.0) + jnp.log1p(jnp.exp(-jnp.abs(z)))
    decay = jnp.exp(-jnp.exp(-softplus - 0.5))
    a = jax.nn.sigmoid(a0_ref[...] + lo[:, d_a:])
    g = jnp.dot(jax.nn.sigmoid(gl).astype(BF16), g2_ref[...], preferred_element_type=F32)
    bd = bd_ref[...]
    kk = k * kk_ref[...]
    kk = kk / jnp.maximum(jnp.sqrt(_group_sum(kk * kk, bd)), 1e-12)
    kh = k * (1.0 + (a - 1.0) * ka_ref[...])
    r_o[...] = r
    k_o[...] = kh
    v_o[...] = v
    w_o[...] = decay
    a_o[...] = -kk
    b_o[...] = kk * a
    g_o[...] = g
    bonus_o[...] = _group_sum(r * kh * rk_ref[...], bd) * v


def _prep(pa, shift_prev, seq, mu, w0, a0, wa2_bf, g2_bf, k_k, k_a, r_k, bd):
    t, p_a = pa.shape
    batch = t // seq
    d_a = w0.shape[1]
    if seq == 1:
        tm = _row_tile(t, 256)
        grid = (t // tm, 1)
        prev, prev_spec, scratch = shift_prev, pl.BlockSpec((tm, p_a), lambda i, l: (i, 0)), []
    else:
        tm = _row_tile(seq, 256)
        grid = (batch, seq // tm)
        prev = shift_prev.reshape(batch, 1, p_a)
        prev_spec, scratch = pl.BlockSpec((1, 1, p_a), lambda b, l: (b, 0, 0)), [pltpu.VMEM((1, p_a), F32)]
    n_l = grid[1]
    row = lambda n: pl.BlockSpec((tm, n), lambda b, l: (b * n_l + l, 0))
    const = lambda shape: pl.BlockSpec(shape, lambda b, l: (0,) * len(shape))
    return pl.pallas_call(
        functools.partial(_prep_kernel, d_a=d_a),
        grid=grid,
        in_specs=[row(p_a), prev_spec, const((1, p_a)), const((1, d_a)), const((1, d_a)),
                  const(wa2_bf.shape), const(g2_bf.shape), const((1, d_a)), const((1, d_a)),
                  const((1, d_a)), const(bd.shape)],
        out_specs=[row(d_a)] * 8,
        out_shape=[jax.ShapeDtypeStruct((t, d_a), F32)] * 8,
        scratch_shapes=scratch,
        compiler_params=_cparams("parallel", "arbitrary"),
        name="rwkv_prep",
    )(pa, prev, mu, w0, a0, wa2_bf, g2_bf, k_k, k_a, r_k, bd)


def _recurrence(st_ref, w_ref, a_ref, b_ref, k_ref, r_ref, v_ref, y_ref, tl, nv):
    def step(t, carry):
        for i in range(nv):
            s = st_ref[i]
            sa = jnp.sum(s * a_ref[t], axis=0, keepdims=True)
            s = s * w_ref[t] + sa * b_ref[t] + v_ref[t, pl.ds(i, 1), :] * k_ref[t]
            st_ref[i] = s
            y_ref[t, pl.ds(i, 1), :] = jnp.sum(s * r_ref[t], axis=0, keepdims=True)
        return carry

    lax.fori_loop(0, tl, step, 0)


def _scan_kernel(w_ref, a_ref, b_ref, k_ref, r_ref, v_ref, s0_ref, y_ref, sfin_ref, st_ref, *, tl, nv):
    l = pl.program_id(1)

    @pl.when(l == 0)
    def _():
        st_ref[...] = s0_ref[...]

    _recurrence(st_ref, w_ref, a_ref, b_ref, k_ref, r_ref, v_ref, y_ref, tl, nv)

    @pl.when(l == pl.num_programs(1) - 1)
    def _():
        sfin_ref[...] = st_ref[...]


def _scan(w, a, b, k, r, v, s0):
    l, n, gl = w.shape
    nv = v.shape[1]
    tl = _row_tile(l, 64)
    rows = pl.BlockSpec((tl, n, LANES), lambda g, i: (i, 0, g))
    vals = pl.BlockSpec((tl, nv, LANES), lambda g, i: (i, 0, g))
    state = pl.BlockSpec((nv, n, LANES), lambda g, i: (0, 0, g))
    return pl.pallas_call(
        functools.partial(_scan_kernel, tl=tl, nv=nv),
        grid=(gl // LANES, l // tl),
        in_specs=[rows] * 5 + [vals, state],
        out_specs=[vals, state],
        out_shape=[jax.ShapeDtypeStruct((l, nv, gl), F32), jax.ShapeDtypeStruct((nv, n, gl), F32)],
        scratch_shapes=[pltpu.VMEM((nv, n, LANES), F32)],
        compiler_params=_cparams("parallel", "arbitrary"),
        name="rwkv_scan",
    )(w, a, b, k, r, v, s0)


def _keys_to_lanes(x, n_heads):
    pieces = [x[:, h * HEAD:(h + 1) * HEAD] for h in range(n_heads)]
    return jnp.concatenate(pieces + pieces, axis=0).T


def _values_to_lanes(x, n_heads):
    half = HEAD // 2
    lo = [x[:, h * HEAD:h * HEAD + half] for h in range(n_heads)]
    hi = [x[:, h * HEAD + half:(h + 1) * HEAD] for h in range(n_heads)]
    return jnp.concatenate(lo + hi, axis=0).T


def _values_from_lanes(y, n_heads, batch):
    yt = y.T
    hb = n_heads * batch
    full = jnp.concatenate([yt[0:hb], yt[hb:2 * hb]], axis=1)
    return jnp.concatenate([full[h * batch:(h + 1) * batch] for h in range(n_heads)], axis=1)


N_KEY_ARRAYS = 5


def _scan_tokens_kernel(*refs, tl, nv, n_heads, batch):
    key_refs, v_ref = refs[:N_KEY_ARRAYS], refs[N_KEY_ARRAYS]
    y_ref, sfin_ref, st_ref, sa_ref = refs[N_KEY_ARRAYS + 1:N_KEY_ARRAYS + 5]
    slot_refs = refs[N_KEY_ARRAYS + 5:]
    n_slot = N_KEY_ARRAYS + 2
    slots = (slot_refs[:n_slot], slot_refs[n_slot:])
    l = pl.program_id(0)

    @pl.when(l == 0)
    def _():
        st_ref[...] = jnp.zeros_like(st_ref)
        slots[1][-1][...] = jnp.zeros_like(slots[1][-1])

    def fill(slot, t):
        for src, dst in zip(key_refs, slot[:N_KEY_ARRAYS]):
            dst[...] = _keys_to_lanes(src[:, t, :], n_heads)
        slot[N_KEY_ARRAYS][...] = _values_to_lanes(v_ref[:, t, :], n_heads)

    fill(slots[0], 0)
    spread = nv // (n_slot + 1)

    def one_step(cur, other, t_next, t_prev):
        wt, at, bt, kt, rt, vt, yt = cur

        def relayout(job):
            if job == 0:
                y_ref[:, t_prev, :] = _values_from_lanes(other[-1][...], n_heads, batch)
            elif job <= N_KEY_ARRAYS:
                other[job - 1][...] = _keys_to_lanes(key_refs[job - 1][:, t_next, :], n_heads)
            elif job == N_KEY_ARRAYS + 1:
                other[N_KEY_ARRAYS][...] = _values_to_lanes(v_ref[:, t_next, :], n_heads)

        for i in range(nv):
            sa_ref[pl.ds(i, 1), :] = jnp.sum(st_ref[i] * at[...], axis=0, keepdims=True)
            if i < n_slot:
                relayout(i)
        for i in range(nv):
            s = st_ref[i] * wt[...] + sa_ref[pl.ds(i, 1), :] * bt[...] + vt[pl.ds(i, 1), :] * kt[...]
            st_ref[i] = s
            yt[pl.ds(i, 1), :] = jnp.sum(s * rt[...], axis=0, keepdims=True)

    def two_steps(j, carry):
        t = 2 * j
        one_step(slots[0], slots[1], t + 1, jnp.maximum(t - 1, 0))
        one_step(slots[1], slots[0], jnp.minimum(t + 2, tl - 1), t)
        return carry

    lax.fori_loop(0, tl // 2, two_steps, 0)
    y_ref[:, tl - 1, :] = _values_from_lanes(slots[1][-1][...], n_heads, batch)

    @pl.when(l == pl.num_programs(0) - 1)
    def _():
        sfin_ref[...] = st_ref[...]


def _scan_tokens(w, a, b, k, r, v):
    batch, l, d = w.shape
    n_heads = d // HEAD
    assert 2 * batch * n_heads == LANES
    nv = HEAD // 2
    tl = _row_tile(l, 64)
    tok = pl.BlockSpec((batch, tl, d), lambda i: (0, i, 0))
    state = pl.BlockSpec((nv, HEAD, LANES), lambda i: (0, 0, 0))
    assert tl % 2 == 0
    slot = [pltpu.VMEM((HEAD, LANES), F32)] * N_KEY_ARRAYS + [pltpu.VMEM((nv, LANES), F32)] * 2
    return pl.pallas_call(
        functools.partial(_scan_tokens_kernel, tl=tl, nv=nv, n_heads=n_heads, batch=batch),
        grid=(l // tl,),
        in_specs=[tok] * 6,
        out_specs=[tok, state],
        out_shape=[jax.ShapeDtypeStruct((batch, l, d), F32), jax.ShapeDtypeStruct((nv, HEAD, LANES), F32)],
        scratch_shapes=[pltpu.VMEM((nv, HEAD, LANES), F32), pltpu.VMEM((nv, LANES), F32)] + slot + slot,
        compiler_params=_cparams("arbitrary"),
        name="rwkv_scan_tokens",
    )(w, a, b, k, r, v)


def _cmlp_norm(pc, lng, lnb, bd, d_c):
    ge = _gelu(pc)
    u = ge[:, :d_c]
    v = ge[:, d_c:]
    mean = _group_sum(v, bd) * (1.0 / HEAD)
    d = v - mean
    var = _group_sum(d * d, bd) * (1.0 / HEAD)
    return u, d * lax.rsqrt(var + LN_EPS) * lng + lnb


def _cmlp_kernel(pc_ref, lng_ref, lnb_ref, ws_ref, bs_ref, bd_ref, yc_ref, *, d_c, n_chunks):
    u, vn = _cmlp_norm(pc_ref[...], lng_ref[...], lnb_ref[...], bd_ref[...], d_c)
    vb = vn.astype(BF16)
    ri = lax.broadcasted_iota(jnp.int32, (CHUNK, CHUNK), 0)
    ci = lax.broadcasted_iota(jnp.int32, (CHUNK, CHUNK), 1)
    lane = lax.broadcasted_iota(jnp.int32, (CHUNK, LANES), 1)
    n_groups = d_c // HEAD
    wsm = [jnp.where(ri >= ci, ws_ref[g], 0.0).astype(BF16) for g in range(n_groups)]
    for c in range(n_chunks):
        rows = slice(c * CHUNK, (c + 1) * CHUNK)
        for p in range(n_groups // 2):
            cols = slice(p * LANES, (p + 1) * LANES)
            vp = vb[rows, cols]
            s0 = jnp.dot(wsm[2 * p], vp, preferred_element_type=F32)
            s1 = jnp.dot(wsm[2 * p + 1], vp, preferred_element_type=F32)
            s = jnp.where(lane < HEAD, s0, s1) + bs_ref[:, cols]
            yc_ref[rows, cols] = u[rows, cols] * s


def _cmlp(pc, lng, lnb, ws, bs_rows, bd):
    t, p_c = pc.shape
    d_c = p_c // 2
    tm = _row_tile(t, 256)
    assert tm % CHUNK == 0
    return pl.pallas_call(
        functools.partial(_cmlp_kernel, d_c=d_c, n_chunks=tm // CHUNK),
        grid=(t // tm,),
        in_specs=[pl.BlockSpec((tm, p_c), lambda i: (i, 0)), _full((1, d_c)), _full((1, d_c)),
                  _full(ws.shape), _full(bs_rows.shape), _full(bd.shape)],
        out_specs=pl.BlockSpec((tm, d_c), lambda i: (i, 0)),
        out_shape=jax.ShapeDtypeStruct((t, d_c), F32),
        compiler_params=_cparams("parallel"),
        name="cmlp",
    )(pc, lng, lnb, ws, bs_rows, bd)


def _cmlp_first_kernel(pc_ref, lng_ref, lnb_ref, w00_ref, b0_ref, bd_ref, yc_ref, vn_ref, *, d_c):
    u, vn = _cmlp_norm(pc_ref[...], lng_ref[...], lnb_ref[...], bd_ref[...], d_c)
    yc_ref[...] = u * (w00_ref[...] * vn + b0_ref[...])
    vn_ref[...] = vn


def _cmlp_first(pc, lng, lnb, w00, b0, bd):
    t, p_c = pc.shape
    d_c = p_c // 2
    return pl.pallas_call(
        functools.partial(_cmlp_first_kernel, d_c=d_c),
        grid=(1,),
        in_specs=[_full(pc.shape), _full((1, d_c)), _full((1, d_c)), _full((1, d_c)), _full((1, d_c)),
                  _full(bd.shape)],
        out_specs=[_full((t, d_c))] * 2,
        out_shape=[jax.ShapeDtypeStruct((t, d_c), F32)] * 2,
        compiler_params=_cparams("arbitrary"),
        name="cmlp_first",
    )(pc, lng, lnb, w00, b0, bd)


def _mix_kernel(x_ref, y_ref, bonus_ref, g_ref, yc_ref, lng_ref, lnb_ref, bd_ref, wo_ref, o_ref, *, d_a):
    bd = bd_ref[...]
    y = y_ref[...]
    mean = _group_sum(y, bd) * (1.0 / HEAD)
    d = y - mean
    var = _group_sum(d * d, bd) * (1.0 / HEAD)
    yn = d * lax.rsqrt(var + GN_EPS) * lng_ref[...] + lnb_ref[...]
    ya = (yn + bonus_ref[...]) * g_ref[...]
    o_ref[...] = (x_ref[...]
                  + jnp.dot(ya.astype(BF16), wo_ref[0:d_a, :], preferred_element_type=F32)
                  + jnp.dot(yc_ref[...].astype(BF16), wo_ref[d_a:, :], preferred_element_type=F32))


def _mix(x, y, bonus, g, yc, lng, lnb, bd, wo_bf):
    t, d = x.shape
    d_a = y.shape[1]
    d_c = yc.shape[1]
    tm = _row_tile(t, 256)
    row = lambda n: pl.BlockSpec((tm, n), lambda i: (i, 0))
    return pl.pallas_call(
        functools.partial(_mix_kernel, d_a=d_a),
        grid=(t // tm,),
        in_specs=[row(d), row(d_a), row(d_a), row(d_a), row(d_c), _full((1, d_a)), _full((1, d_a)),
                  _full(bd.shape), _full(wo_bf.shape)],
        out_specs=row(d),
        out_shape=jax.ShapeDtypeStruct((t, d), F32),
        compiler_params=_cparams("parallel"),
        name="mix_out",
    )(x, y, bonus, g, yc, lng, lnb, bd, wo_bf)


def _peerq_kernel(x_ref, g_ref, wq_ref, sk_ref, xn_ref, st_ref, *, n_hc):
    hb = _rmsnorm(x_ref[...], g_ref[...]).astype(BF16)
    xn_ref[...] = hb
    qb = jnp.dot(hb, wq_ref[...], preferred_element_type=F32).astype(BF16)
    for hc in range(n_hc):
        st_ref[hc] = lax.dot_general(sk_ref[hc % 2], qb[:, hc * LANES:(hc + 1) * LANES],
                                     (((1,), (1,)), ((), ())), preferred_element_type=F32)


def _peerq(x, g, wq_bf, sk_bf):
    t, d = x.shape
    n_hc = wq_bf.shape[1] // LANES
    tm = _row_tile(t, 512)
    return pl.pallas_call(
        functools.partial(_peerq_kernel, n_hc=n_hc),
        grid=(t // tm,),
        in_specs=[pl.BlockSpec((tm, d), lambda i: (i, 0)), _full((1, d)), _full(wq_bf.shape),
                  _full(sk_bf.shape)],
        out_specs=[pl.BlockSpec((tm, d), lambda i: (i, 0)),
                   pl.BlockSpec((n_hc, N_KEYS, tm), lambda i: (0, 0, i))],
        out_shape=[jax.ShapeDtypeStruct((t, d), BF16), jax.ShapeDtypeStruct((n_hc, N_KEYS, t), F32)],
        compiler_params=_cparams("parallel"),
        name="peer_query",
    )(x, g, wq_bf, sk_bf)


def _batcher_network(n):
    pairs = []

    def merge(lo, m, r):
        step = 2 * r
        if step < m:
            merge(lo, m, step)
            merge(lo + r, m, step)
            pairs.extend((i, i + r) for i in range(lo + r, lo + m - r, step))
        else:
            pairs.append((lo, lo + r))

    def sort(lo, m):
        if m > 1:
            sort(lo, m // 2)
            sort(lo + m // 2, m // 2)
            merge(lo, m, 1)

    sort(0, n)
    return pairs


SUBLANES = 8
KEY_SLABS = N_KEYS // SUBLANES
SLAB_NETWORK = _batcher_network(KEY_SLABS)


def _pop_heads(lists, extra_heads, n):
    lists, extra_heads = list(lists), list(extra_heads)
    tops = []
    for r in range(n):
        head = lists[0]
        for e in extra_heads:
            head = jnp.maximum(head, e)
        m = jnp.max(head, axis=0, keepdims=True)
        tops.append(m)
        hit = lists[0] == m
        for d in range(min(n - 1 - r, len(lists))):
            nxt = lists[d + 1] if d + 1 < len(lists) else NEG_INF
            lists[d] = jnp.where(hit, nxt, lists[d])
        if r < n - 1:
            extra_heads = [jnp.where(e == m, NEG_INF, e) for e in extra_heads]
    return tops


def _top_sorted(s, n):
    slabs = [s[SUBLANES * r:SUBLANES * (r + 1)] for r in range(KEY_SLABS)]
    for i, j in SLAB_NETWORK:
        slabs[i], slabs[j] = jnp.maximum(slabs[i], slabs[j]), jnp.minimum(slabs[i], slabs[j])
    return _pop_heads(slabs, [], n)


def _peer_topk_kernel(st_ref, p1_ref, p2_ref, th_ref, *, n_heads):
    k = PEER_TOPK
    n = k + 1
    n_pad = -(-n // SUBLANES) * SUBLANES

    def head(h, carry):
        s1 = st_ref[2 * h]
        s2 = st_ref[2 * h + 1]
        t1 = _top_sorted(s1, n)
        t2 = _top_sorted(s2, n)
        row = lax.broadcasted_iota(jnp.int32, (n_pad, s1.shape[1]), 0)
        v1 = jnp.full((n_pad, s1.shape[1]), NEG_INF, F32)
        for i in range(n):
            v1 = jnp.where(row == i, t1[i], v1)
        row8 = row[0:SUBLANES]
        lists = [v1[0:SUBLANES] + t2[0]]
        for j in range(1, n):
            lists.append(jnp.where(row8 < n // (j + 1), v1[0:SUBLANES] + t2[j], NEG_INF))
        singles = [v1[SUBLANES * b:SUBLANES * (b + 1)] + t2[0] for b in range(1, n_pad // SUBLANES)]
        c = _pop_heads(lists, singles, n)
        m = c[0]
        z = jnp.ones_like(m)
        for r in range(1, k):
            z = z + jnp.exp(c[r] - m)
        inv_z = 1.0 / z
        th_ref[pl.ds(h, 1), :] = jnp.exp(0.5 * (c[k - 1] + c[k]) - m) * inv_z
        p1_ref[h] = jnp.exp(s1 - t1[0]) * inv_z
        p2_ref[h] = jnp.exp(s2 - t2[0])
        return carry

    lax.fori_loop(0, n_heads, head, 0)


def _peer_topk(st):
    n_hc, n, t = st.shape
    n_heads = n_hc // 2
    tk = LANES
    fac = pl.BlockSpec((n_heads, n, tk), lambda i: (0, 0, i))
    return pl.pallas_call(
        functools.partial(_peer_topk_kernel, n_heads=n_heads),
        grid=(t // tk,),
        in_specs=[pl.BlockSpec((n_hc, n, tk), lambda i: (0, 0, i))],
        out_specs=[fac, fac, pl.BlockSpec((n_heads, tk), lambda i: (0, i))],
        out_shape=[jax.ShapeDtypeStruct((n_heads, n, t), F32)] * 2 + [jax.ShapeDtypeStruct((n_heads, t), F32)],
        compiler_params=_cparams("parallel"),
        name="peer_topk",
    )(st)


ROW_BLK = 32
KEYS_PER_TILE = 8


def _peer_dense_kernel(xn_ref, x_ref, u_ref, v_ref, p1_ref, p2_ref, th_ref, o_ref, ht_ref, at_ref, acc_ref,
                       *, n_heads, tm):
    j = pl.program_id(1)

    @pl.when(j == 0)
    def _():
        acc_ref[...] = jnp.zeros_like(acc_ref)

    ht_ref[...] = lax.dot_general(u_ref[...], xn_ref[...], (((1,), (1,)), ((), ())),
                                  preferred_element_type=F32)

    for k in range(KEYS_PER_TILE):
        for lc in range(tm // LANES):
            cols = pl.ds(lc * LANES, LANES)
            p1_rows = [p1_ref[h, k:k + 1, cols] for h in range(n_heads)]
            th_rows = [th_ref[h:h + 1, cols] for h in range(n_heads)]
            for rb in range(N_KEYS // ROW_BLK):
                r2 = pl.ds(rb * ROW_BLK, ROW_BLK)
                r0 = pl.ds(k * N_KEYS + rb * ROW_BLK, ROW_BLK)
                gsum = jnp.zeros((ROW_BLK, LANES), F32)
                for h in range(n_heads):
                    w = p1_rows[h] * p2_ref[h, r2, cols]
                    gsum = gsum + jnp.where(w >= th_rows[h], w, 0.0)
                at_ref[r0, cols] = (_gelu(ht_ref[r0, cols]) * gsum).astype(BF16)

    acc_ref[...] += lax.dot_general(at_ref[...], v_ref[...], (((0,), (0,)), ((), ())),
                                    preferred_element_type=F32)

    @pl.when(j == pl.num_programs(1) - 1)
    def _():
        o_ref[...] = x_ref[...] + acc_ref[...]


def _peer_dense(xn, x, u_bf, v_bf, p1, p2, th):
    t, d = x.shape
    n_exp = u_bf.shape[0]
    n_heads = p1.shape[0]
    tm = _row_tile(t, 512)
    te = KEYS_PER_TILE * N_KEYS
    assert n_exp == N_KEYS * N_KEYS
    return pl.pallas_call(
        functools.partial(_peer_dense_kernel, n_heads=n_heads, tm=tm),
        grid=(t // tm, n_exp // te),
        in_specs=[pl.BlockSpec((tm, d), lambda i, j: (i, 0)),
                  pl.BlockSpec((tm, d), lambda i, j: (i, 0)),
                  pl.BlockSpec((te, d), lambda i, j: (j, 0)),
                  pl.BlockSpec((te, d), lambda i, j: (j, 0)),
                  pl.BlockSpec((n_heads, KEYS_PER_TILE, tm), lambda i, j: (0, j, i)),
                  pl.BlockSpec((n_heads, N_KEYS, tm), lambda i, j: (0, 0, i)),
                  pl.BlockSpec((n_heads, tm), lambda i, j: (0, i))],
        out_specs=pl.BlockSpec((tm, d), lambda i, j: (i, 0)),
        out_shape=jax.ShapeDtypeStruct((t, d), F32),
        scratch_shapes=[pltpu.VMEM((te, tm), F32), pltpu.VMEM((te, tm), BF16), pltpu.VMEM((tm, d), F32)],
        compiler_params=_cparams("parallel", "arbitrary"),
        name="peer_dense",
    )(xn, x, u_bf, v_bf, p1, p2, th)


def _ple_kernel(x_ref, p_ref, g_ref, gw_ref, pw_ref, gf_ref, o_ref, *, final):
    x = x_ref[...]
    hn = _rmsnorm(x, g_ref[...])
    gate = jax.nn.sigmoid(jnp.dot(hn.astype(BF16), gw_ref[...], preferred_element_type=F32))
    e = jnp.dot(p_ref[...].astype(BF16), pw_ref[...], preferred_element_type=F32)
    x = x + e * gate
    o_ref[...] = _rmsnorm(x, gf_ref[...]) if final else x


def _ple(x, p, g, gw_bf, pw_bf, gf, final):
    t, d = x.shape
    dp = p.shape[1]
    tm = _row_tile(t, 512)
    return pl.pallas_call(
        functools.partial(_ple_kernel, final=final),
        grid=(t // tm,),
        in_specs=[pl.BlockSpec((tm, d), lambda i: (i, 0)), pl.BlockSpec((tm, dp), lambda i: (i, 0)),
                  _full((1, d)), _full(gw_bf.shape), _full(pw_bf.shape), _full((1, d))],
        out_specs=pl.BlockSpec((tm, d), lambda i: (i, 0)),
        out_shape=jax.ShapeDtypeStruct((t, d), F32),
        compiler_params=_cparams("parallel"),
        name="ple",
    )(x, p, g, gw_bf, pw_bf, gf)


def _rwkv_prompt_scan(parts, batch, seq, n_heads):
    r, k, v, w, a, b = (x.reshape(batch, seq, n_heads * HEAD) for x in parts)
    y, s_fin = _scan_tokens(w, a, b, k, r, v)
    s_fin = s_fin.reshape(HEAD // 2, HEAD, 2, n_heads, batch).transpose(4, 3, 2, 0, 1)
    return y.reshape(batch * seq, n_heads * HEAD), s_fin.reshape(batch, n_heads, HEAD, HEAD)


def _rwkv_sample_scan(parts, wkv_prev, batch, n_heads):
    bh = batch * n_heads
    assert bh % LANES == 0
    rows = lambda x: x.reshape(batch, n_heads, HEAD).transpose(2, 0, 1).reshape(1, HEAD, bh)
    r, k, v, w, a, b = parts
    s0 = wkv_prev.transpose(2, 3, 0, 1).reshape(HEAD, HEAD, bh)
    y, s_fin = _scan(rows(w), rows(a), rows(b), rows(k), rows(r), rows(v), s0)
    y = y.reshape(HEAD, batch, n_heads).transpose(1, 2, 0).reshape(batch, n_heads * HEAD)
    return y, s_fin.reshape(HEAD, HEAD, batch, n_heads).transpose(2, 3, 0, 1)


def _layer(x, p, wkv_prev, shift_prev, lw, gf, final):
    batch, seq, d = x.shape
    t = batch * seq
    d_a = lw["w0"].shape[1]
    n_heads = d_a // HEAD
    p_a = lw["mu"].shape[1]
    xt = x.reshape(t, d)

    pa, pc = _proj(xt, lw["norm_mix_g"], lw["w_in"], p_a)
    pa3 = pa.reshape(batch, seq, p_a)
    r, k, v, w, a, b, g, bonus = _prep(pa, shift_prev, seq, lw["mu"], lw["w0"], lw["a0"], lw["wa2"], lw["g2"],
                                       lw["k_k"], lw["k_a"], lw["r_k"], lw["bd"])
    if wkv_prev is None:
        y, wkv_new = _rwkv_prompt_scan((r, k, v, w, a, b), batch, seq, n_heads)
        yc = _cmlp(pc, lw["cmlp_ln_g"], lw["cmlp_ln_b"], lw["cmlp_ws"], lw["cmlp_bs_rows"], lw["bd"])
        v_rows = None
    else:
        assert seq == 1
        y, wkv_new = _rwkv_sample_scan((r, k, v, w, a, b), wkv_prev, batch, n_heads)
        yc, v_rows = _cmlp_first(pc, lw["cmlp_ln_g"], lw["cmlp_ln_b"], lw["cmlp_w00"], lw["cmlp_b0"], lw["bd"])
        v_rows = v_rows.reshape(batch, seq, -1)
    x1 = _mix(xt, y, bonus, g, yc, lw["rwkv_ln_g"], lw["rwkv_ln_b"], lw["bd"], lw["w_out"])

    xn, st = _peerq(x1, lw["norm_ffn_g"], lw["peer_wq"], lw["peer_subkeys"])
    p1, p2, th = _peer_topk(st)
    x2 = _peer_dense(xn, x1, lw["peer_u"], lw["peer_v"], p1, p2, th)

    x3 = _ple(x2, p.reshape(t, -1), lw["norm_ple_g"], lw["ple_gate_w"], lw["ple_w"], gf, final)
    return x3.reshape(batch, seq, d), wkv_new, pa3[:, -1], v_rows


def kernel(x_prompt, x_sample, state_wkv, state_shift, p_prompt, p_sample, norm_mix_g, w_in, shift_mu, rwkv_w0, rwkv_w2, rwkv_a0, rwkv_a2, rwkv_g2, rwkv_k_k, rwkv_k_a, rwkv_r_k, rwkv_ln_g, rwkv_ln_b, cmlp_ln_g, cmlp_ln_b, cmlp_ws, cmlp_bs, w_out, norm_ffn_g, peer_wq, peer_subkeys, peer_u, peer_v, norm_ple_g, ple_w, ple_gate_w, norm_final_g):
    depth = state_wkv.shape[0]
    batch = x_prompt.shape[0]
    d_a = rwkv_w0.shape[1]
    d_c = cmlp_ln_g.shape[1]
    p_a = shift_mu.shape[1]
    lora_w = rwkv_w2.shape[1]
    lora_a = rwkv_a2.shape[1]
    assert lora_w == HEAD and lora_a == HEAD and d_a == d_c
    gidx = jnp.arange(d_a) // HEAD
    bd = (gidx[:, None] == gidx[None, :]).astype(F32)
    row = lambda z: z.reshape(1, -1)
    gf = row(norm_final_g)

    hp, hs = x_prompt, x_sample
    outs = [[] for _ in range(5)]
    for i in range(depth):
        wa2 = jnp.zeros((lora_w + lora_a, 2 * d_a), F32)
        wa2 = wa2.at[:lora_w, :d_a].set(rwkv_w2[i]).at[lora_w:, d_a:].set(rwkv_a2[i])
        lw = dict(
            norm_mix_g=row(norm_mix_g[i]), w_in=w_in[i].astype(BF16), mu=row(shift_mu[i]),
            w0=row(rwkv_w0[i]), a0=row(rwkv_a0[i]), wa2=wa2.astype(BF16), g2=rwkv_g2[i].astype(BF16),
            k_k=row(rwkv_k_k[i]), k_a=row(rwkv_k_a[i]), r_k=row(rwkv_r_k[i]), bd=bd,
            rwkv_ln_g=row(rwkv_ln_g[i]), rwkv_ln_b=row(rwkv_ln_b[i]),
            cmlp_ln_g=row(cmlp_ln_g[i]), cmlp_ln_b=row(cmlp_ln_b[i]), cmlp_ws=cmlp_ws[i],
            cmlp_bs_rows=jnp.repeat(cmlp_bs[i].T, HEAD, axis=1),
            cmlp_w00=row(jnp.repeat(cmlp_ws[i][:, 0, 0], HEAD)), cmlp_b0=row(jnp.repeat(cmlp_bs[i][:, 0], HEAD)),
            w_out=w_out[i].astype(BF16), norm_ffn_g=row(norm_ffn_g[i]), peer_wq=peer_wq[i].astype(BF16),
            peer_subkeys=peer_subkeys[i].astype(BF16), peer_u=peer_u[i].astype(BF16),
            peer_v=peer_v[i].astype(BF16), norm_ple_g=row(norm_ple_g[i]),
            ple_w=ple_w[i].astype(BF16), ple_gate_w=ple_gate_w[i].astype(BF16),
        )
        final = i == depth - 1
        shift0 = jnp.zeros((batch, p_a), F32)
        hp, wkv_p, shift_p, _ = _layer(hp, p_prompt[i], None, shift0, lw, gf, final)
        hs, wkv_s, shift_s, v_s = _layer(hs, p_sample[i], state_wkv[i], state_shift[i], lw, gf, final)
        for o, val in zip(outs, (wkv_p, shift_p, wkv_s, shift_s, v_s)):
            o.append(val)
    return (hp, hs) + tuple(jnp.stack(o) for o in outs)
```

```python
import functools
import math

import jax
import jax.numpy as jnp
from jax import lax
from jax.experimental import pallas as pl
from jax.experimental.pallas import tpu as pltpu

F32 = jnp.float32
BF16 = jnp.bfloat16

LANES = 128
HEAD = 64
CHUNK = 128
N_KEYS = 128
PEER_TOPK = 16
RMS_EPS = 1e-6
LN_EPS = 1e-5
GN_EPS = 64e-5
SQRT_HALF = math.sqrt(0.5)
NEG_INF = float("-inf")
VMEM_LIMIT = 56 * 1024 * 1024


def _cparams(*sem):
    return pltpu.CompilerParams(dimension_semantics=sem, vmem_limit_bytes=VMEM_LIMIT)


def _rmsnorm(x, g):
    return x * lax.rsqrt(jnp.mean(x * x, axis=-1, keepdims=True) + RMS_EPS) * g


def _gelu(x):
    return 0.5 * x * (1.0 + lax.erf(x * SQRT_HALF))


def _group_sum(x, bd):
    hi = x.astype(BF16)
    rest = x - hi.astype(F32)
    mid = rest.astype(BF16)
    lo = (rest - mid.astype(F32)).astype(BF16)
    return (jnp.dot(hi, bd, preferred_element_type=F32) + jnp.dot(mid, bd, preferred_element_type=F32)
            + jnp.dot(lo, bd, preferred_element_type=F32))


def _row_tile(n, want):
    t = min(n, want)
    assert n % t == 0, (n, t)
    return t


def _full(shape):
    return pl.BlockSpec(shape, lambda *_: (0,) * len(shape))


def _proj_kernel(x_ref, g_ref, w_ref, pa_ref, pc_ref, *, p_a):
    h = _rmsnorm(x_ref[...], g_ref[...])
    p = jnp.dot(h.astype(BF16), w_ref[...], preferred_element_type=F32)
    pa_ref[...] = p[:, :p_a]
    pc_ref[...] = p[:, p_a:]


def _proj(x, g, w_bf, p_a):
    t, d = x.shape
    p_tot = w_bf.shape[1]
    tm = _row_tile(t, 512)
    return pl.pallas_call(
        functools.partial(_proj_kernel, p_a=p_a),
        grid=(t // tm,),
        in_specs=[pl.BlockSpec((tm, d), lambda i: (i, 0)), _full((1, d)), _full((d, p_tot))],
        out_specs=[pl.BlockSpec((tm, p_a), lambda i: (i, 0)),
                   pl.BlockSpec((tm, p_tot - p_a), lambda i: (i, 0))],
        out_shape=[jax.ShapeDtypeStruct((t, p_a), F32), jax.ShapeDtypeStruct((t, p_tot - p_a), F32)],
        compiler_params=_cparams("parallel"),
        name="proj",
    )(x, g, w_bf)


def _prep_kernel(pa_ref, prev_ref, mu_ref, w0_ref, a0_ref, wa2_ref, g2_ref, kk_ref, ka_ref, rk_ref, bd_ref,
                 r_o, k_o, v_o, w_o, a_o, b_o, g_o, bonus_o, *carry, d_a):
    pa = pa_ref[...]
    if carry:
        (carry_ref,) = carry
        first = jnp.where(pl.program_id(1) == 0, prev_ref[0], carry_ref[...])
        row = lax.broadcasted_iota(jnp.int32, pa.shape, 0)
        prev = jnp.where(row == 0, first, pltpu.roll(pa, 1, 0))
        carry_ref[...] = pa[pa.shape[0] - 1:, :]
    else:
        prev = prev_ref[...]
    xm = pa + (prev - pa) * mu_ref[...]
    r = xm[:, 0:d_a]
    k = xm[:, d_a:2 * d_a]
    v = xm[:, 2 * d_a:3 * d_a]
    wal = xm[:, 3 * d_a:3 * d_a + LANES]
    gl = xm[:, 3 * d_a + LANES:]
    lane = lax.broadcasted_iota(jnp.int32, wal.shape, 1)
    wal = jnp.where(lane < HEAD, jnp.tanh(wal), wal)
    lo = jnp.dot(wal.astype(BF16), wa2_ref[...], preferred_element_type=F32)
    z = -(w0_ref[...] + lo[:, :d_a])
    softplus = jnp.maximum(z, 0.0) + jnp.log1p(jnp.exp(-jnp.abs(z)))
    decay = jnp.exp(-jnp.exp(-softplus - 0.5))
    a = jax.nn.sigmoid(a0_ref[...] + lo[:, d_a:])
    g = jnp.dot(jax.nn.sigmoid(gl).astype(BF16), g2_ref[...], preferred_element_type=F32)
    bd = bd_ref[...]
    kk = k * kk_ref[...]
    kk = kk / jnp.maximum(jnp.sqrt(_group_sum(kk * kk, bd)), 1e-12)
    kh = k * (1.0 + (a - 1.0) * ka_ref[...])
    r_o[...] = r
    k_o[...] = kh
    v_o[...] = v
    w_o[...] = decay
    a_o[...] = -kk
    b_o[...] = kk * a
    g_o[...] = g
    bonus_o[...] = _group_sum(r * kh * rk_ref[...], bd) * v


def _prep(pa, shift_prev, seq, mu, w0, a0, wa2_bf, g2_bf, k_k, k_a, r_k, bd):
    t, p_a = pa.shape
    batch = t // seq
    d_a = w0.shape[1]
    if seq == 1:
        tm = _row_tile(t, 256)
        grid = (t // tm, 1)
        prev, prev_spec, scratch = shift_prev, pl.BlockSpec((tm, p_a), lambda i, l: (i, 0)), []
    else:
        tm = _row_tile(seq, 256)
        grid = (batch, seq // tm)
        prev = shift_prev.reshape(batch, 1, p_a)
        prev_spec, scratch = pl.BlockSpec((1, 1, p_a), lambda b, l: (b, 0, 0)), [pltpu.VMEM((1, p_a), F32)]
    n_l = grid[1]
    row = lambda n: pl.BlockSpec((tm, n), lambda b, l: (b * n_l + l, 0))
    const = lambda shape: pl.BlockSpec(shape, lambda b, l: (0,) * len(shape))
    return pl.pallas_call(
        functools.partial(_prep_kernel, d_a=d_a),
        grid=grid,
        in_specs=[row(p_a), prev_spec, const((1, p_a)), const((1, d_a)), const((1, d_a)),
                  const(wa2_bf.shape), const(g2_bf.shape), const((1, d_a)), const((1, d_a)),
                  const((1, d_a)), const(bd.shape)],
        out_specs=[row(d_a)] * 8,
        out_shape=[jax.ShapeDtypeStruct((t, d_a), F32)] * 8,
        scratch_shapes=scratch,
        compiler_params=_cparams("parallel", "arbitrary"),
        name="rwkv_prep",
    )(pa, prev, mu, w0, a0, wa2_bf, g2_bf, k_k, k_a, r_k, bd)


def _recurrence(st_ref, w_ref, a_ref, b_ref, k_ref, r_ref, v_ref, y_ref, tl, nv):
    def step(t, carry):
        for i in range(nv):
            s = st_ref[i]
            sa = jnp.sum(s * a_ref[t], axis=0, keepdims=True)
            s = s * w_ref[t] + sa * b_ref[t] + v_ref[t, pl.ds(i, 1), :] * k_ref[t]
            st_ref[i] = s
            y_ref[t, pl.ds(i, 1), :] = jnp.sum(s * r_ref[t], axis=0, keepdims=True)
        return carry

    lax.fori_loop(0, tl, step, 0)


def _scan_kernel(w_ref, a_ref, b_ref, k_ref, r_ref, v_ref, s0_ref, y_ref, sfin_ref, st_ref, *, tl, nv):
    l = pl.program_id(1)

    @pl.when(l == 0)
    def _():
        st_ref[...] = s0_ref[...]

    _recurrence(st_ref, w_ref, a_ref, b_ref, k_ref, r_ref, v_ref, y_ref, tl, nv)

    @pl.when(l == pl.num_programs(1) - 1)
    def _():
        sfin_ref[...] = st_ref[...]


def _scan(w, a, b, k, r, v, s0):
    l, n, gl = w.shape
    nv = v.shape[1]
    tl = _row_tile(l, 64)
    rows = pl.BlockSpec((tl, n, LANES), lambda g, i: (i, 0, g))
    vals = pl.BlockSpec((tl, nv, LANES), lambda g, i: (i, 0, g))
    state = pl.BlockSpec((nv, n, LANES), lambda g, i: (0, 0, g))
    return pl.pallas_call(
        functools.partial(_scan_kernel, tl=tl, nv=nv),
        grid=(gl // LANES, l // tl),
        in_specs=[rows] * 5 + [vals, state],
        out_specs=[vals, state],
        out_shape=[jax.ShapeDtypeStruct((l, nv, gl), F32), jax.ShapeDtypeStruct((nv, n, gl), F32)],
        scratch_shapes=[pltpu.VMEM((nv, n, LANES), F32)],
        compiler_params=_cparams("parallel", "arbitrary"),
        name="rwkv_scan",
    )(w, a, b, k, r, v, s0)


def _keys_to_lanes(x, n_heads):
    pieces = [x[:, h * HEAD:(h + 1) * HEAD] for h in range(n_heads)]
    return jnp.concatenate(pieces + pieces, axis=0).T


def _values_to_lanes(x, n_heads):
    half = HEAD // 2
    lo = [x[:, h * HEAD:h * HEAD + half] for h in range(n_heads)]
    hi = [x[:, h * HEAD + half:(h + 1) * HEAD] for h in range(n_heads)]
    return jnp.concatenate(lo + hi, axis=0).T


def _values_from_lanes(y, n_heads, batch):
    yt = y.T
    hb = n_heads * batch
    full = jnp.concatenate([yt[0:hb], yt[hb:2 * hb]], axis=1)
    return jnp.concatenate([full[h * batch:(h + 1) * batch] for h in range(n_heads)], axis=1)


N_KEY_ARRAYS = 5


def _scan_tokens_kernel(*refs, tl, nv, n_heads, batch):
    key_refs, v_ref = refs[:N_KEY_ARRAYS], refs[N_KEY_ARRAYS]
    y_ref, sfin_ref, st_ref, sa_ref = refs[N_KEY_ARRAYS + 1:N_KEY_ARRAYS + 5]
    slot_refs = refs[N_KEY_ARRAYS + 5:]
    n_slot = N_KEY_ARRAYS + 2
    slots = (slot_refs[:n_slot], slot_refs[n_slot:])
    l = pl.program_id(0)

    @pl.when(l == 0)
    def _():
        st_ref[...] = jnp.zeros_like(st_ref)
        slots[1][-1][...] = jnp.zeros_like(slots[1][-1])

    def fill(slot, t):
        for src, dst in zip(key_refs, slot[:N_KEY_ARRAYS]):
            dst[...] = _keys_to_lanes(src[:, t, :], n_heads)
        slot[N_KEY_ARRAYS][...] = _values_to_lanes(v_ref[:, t, :], n_heads)

    fill(slots[0], 0)
    spread = nv // (n_slot + 1)

    def one_step(cur, other, t_next, t_prev):
        wt, at, bt, kt, rt, vt, yt = cur

        def relayout(job):
            if job == 0:
                y_ref[:, t_prev, :] = _values_from_lanes(other[-1][...], n_heads, batch)
            elif job <= N_KEY_ARRAYS:
                other[job - 1][...] = _keys_to_lanes(key_refs[job - 1][:, t_next, :], n_heads)
            elif job == N_KEY_ARRAYS + 1:
                other[N_KEY_ARRAYS][...] = _values_to_lanes(v_ref[:, t_next, :], n_heads)

        for i in range(nv):
            sa_ref[pl.ds(i, 1), :] = jnp.sum(st_ref[i] * at[...], axis=0, keepdims=True)
            if i < n_slot:
                relayout(i)
        for i in range(nv):
            s = st_ref[i] * wt[...] + sa_ref[pl.ds(i, 1), :] * bt[...] + vt[pl.ds(i, 1), :] * kt[...]
            st_ref[i] = s
            yt[pl.ds(i, 1), :] = jnp.sum(s * rt[...], axis=0, keepdims=True)

    def two_steps(j, carry):
        t = 2 * j
        one_step(slots[0], slots[1], t + 1, jnp.maximum(t - 1, 0))
        one_step(slots[1], slots[0], jnp.minimum(t + 2, tl - 1), t)
        return carry

    lax.fori_loop(0, tl // 2, two_steps, 0)
    y_ref[:, tl - 1, :] = _values_from_lanes(slots[1][-1][...], n_heads, batch)

    @pl.when(l == pl.num_programs(0) - 1)
    def _():
        sfin_ref[...] = st_ref[...]


def _scan_tokens(w, a, b, k, r, v):
    batch, l, d = w.shape
    n_heads = d // HEAD
    assert 2 * batch * n_heads == LANES
    nv = HEAD // 2
    tl = _row_tile(l, 64)
    tok = pl.BlockSpec((batch, tl, d), lambda i: (0, i, 0))
    state = pl.BlockSpec((nv, HEAD, LANES), lambda i: (0, 0, 0))
    assert tl % 2 == 0
    slot = [pltpu.VMEM((HEAD, LANES), F32)] * N_KEY_ARRAYS + [pltpu.VMEM((nv, LANES), F32)] * 2
    return pl.pallas_call(
        functools.partial(_scan_tokens_kernel, tl=tl, nv=nv, n_heads=n_heads, batch=batch),
        grid=(l // tl,),
        in_specs=[tok] * 6,
        out_specs=[tok, state],
        out_shape=[jax.ShapeDtypeStruct((batch, l, d), F32), jax.ShapeDtypeStruct((nv, HEAD, LANES), F32)],
        scratch_shapes=[pltpu.VMEM((nv, HEAD, LANES), F32), pltpu.VMEM((nv, LANES), F32)] + slot + slot,
        compiler_params=_cparams("arbitrary"),
        name="rwkv_scan_tokens",
    )(w, a, b, k, r, v)


def _cmlp_norm(pc, lng, lnb, bd, d_c):
    ge = _gelu(pc)
    u = ge[:, :d_c]
    v = ge[:, d_c:]
    mean = _group_sum(v, bd) * (1.0 / HEAD)
    d = v - mean
    var = _group_sum(d * d, bd) * (1.0 / HEAD)
    return u, d * lax.rsqrt(var + LN_EPS) * lng + lnb


def _cmlp_kernel(pc_ref, lng_ref, lnb_ref, ws_ref, bs_ref, bd_ref, yc_ref, *, d_c, n_chunks):
    u, vn = _cmlp_norm(pc_ref[...], lng_ref[...], lnb_ref[...], bd_ref[...], d_c)
    vb = vn.astype(BF16)
    ri = lax.broadcasted_iota(jnp.int32, (CHUNK, CHUNK), 0)
    ci = lax.broadcasted_iota(jnp.int32, (CHUNK, CHUNK), 1)
    lane = lax.broadcasted_iota(jnp.int32, (CHUNK, LANES), 1)
    n_groups = d_c // HEAD
    wsm = [jnp.where(ri >= ci, ws_ref[g], 0.0).astype(BF16) for g in range(n_groups)]
    for c in range(n_chunks):
        rows = slice(c * CHUNK, (c + 1) * CHUNK)
        for p in range(n_groups // 2):
            cols = slice(p * LANES, (p + 1) * LANES)
            vp = vb[rows, cols]
            s0 = jnp.dot(wsm[2 * p], vp, preferred_element_type=F32)
            s1 = jnp.dot(wsm[2 * p + 1], vp, preferred_element_type=F32)
            s = jnp.where(lane < HEAD, s0, s1) + bs_ref[:, cols]
            yc_ref[rows, cols] = u[rows, cols] * s


def _cmlp(pc, lng, lnb, ws, bs_rows, bd):
    t, p_c = pc.shape
    d_c = p_c // 2
    tm = _row_tile(t, 256)
    assert tm % CHUNK == 0
    return pl.pallas_call(
        functools.partial(_cmlp_kernel, d_c=d_c, n_chunks=tm // CHUNK),
        grid=(t // tm,),
        in_specs=[pl.BlockSpec((tm, p_c), lambda i: (i, 0)), _full((1, d_c)), _full((1, d_c)),
                  _full(ws.shape), _full(bs_rows.shape), _full(bd.shape)],
        out_specs=pl.BlockSpec((tm, d_c), lambda i: (i, 0)),
        out_shape=jax.ShapeDtypeStruct((t, d_c), F32),
        compiler_params=_cparams("parallel"),
        name="cmlp",
    )(pc, lng, lnb, ws, bs_rows, bd)


def _cmlp_first_kernel(pc_ref, lng_ref, lnb_ref, w00_ref, b0_ref, bd_ref, yc_ref, vn_ref, *, d_c):
    u, vn = _cmlp_norm(pc_ref[...], lng_ref[...], lnb_ref[...], bd_ref[...], d_c)
    yc_ref[...] = u * (w00_ref[...] * vn + b0_ref[...])
    vn_ref[...] = vn


def _cmlp_first(pc, lng, lnb, w00, b0, bd):
    t, p_c = pc.shape
    d_c = p_c // 2
    return pl.pallas_call(
        functools.partial(_cmlp_first_kernel, d_c=d_c),
        grid=(1,),
        in_specs=[_full(pc.shape), _full((1, d_c)), _full((1, d_c)), _full((1, d_c)), _full((1, d_c)),
                  _full(bd.shape)],
        out_specs=[_full((t, d_c))] * 2,
        out_shape=[jax.ShapeDtypeStruct((t, d_c), F32)] * 2,
        compiler_params=_cparams("arbitrary"),
        name="cmlp_first",
    )(pc, lng, lnb, w00, b0, bd)


def _mix_kernel(x_ref, y_ref, bonus_ref, g_ref, yc_ref, lng_ref, lnb_ref, bd_ref, wo_ref, o_ref, *, d_a):
    bd = bd_ref[...]
    y = y_ref[...]
    mean = _group_sum(y, bd) * (1.0 / HEAD)
    d = y - mean
    var = _group_sum(d * d, bd) * (1.0 / HEAD)
    yn = d * lax.rsqrt(var + GN_EPS) * lng_ref[...] + lnb_ref[...]
    ya = (yn + bonus_ref[...]) * g_ref[...]
    o_ref[...] = (x_ref[...]
                  + jnp.dot(ya.astype(BF16), wo_ref[0:d_a, :], preferred_element_type=F32)
                  + jnp.dot(yc_ref[...].astype(BF16), wo_ref[d_a:, :], preferred_element_type=F32))


def _mix(x, y, bonus, g, yc, lng, lnb, bd, wo_bf):
    t, d = x.shape
    d_a = y.shape[1]
    d_c = yc.shape[1]
    tm = _row_tile(t, 256)
    row = lambda n: pl.BlockSpec((tm, n), lambda i: (i, 0))
    return pl.pallas_call(
        functools.partial(_mix_kernel, d_a=d_a),
        grid=(t // tm,),
        in_specs=[row(d), row(d_a), row(d_a), row(d_a), row(d_c), _full((1, d_a)), _full((1, d_a)),
                  _full(bd.shape), _full(wo_bf.shape)],
        out_specs=row(d),
        out_shape=jax.ShapeDtypeStruct((t, d), F32),
        compiler_params=_cparams("parallel"),
        name="mix_out",
    )(x, y, bonus, g, yc, lng, lnb, bd, wo_bf)


def _peerq_kernel(x_ref, g_ref, wq_ref, sk_ref, xn_ref, st_ref, *, n_hc):
    hb = _rmsnorm(x_ref[...], g_ref[...]).astype(BF16)
    xn_ref[...] = hb
    qb = jnp.dot(hb, wq_ref[...], preferred_element_type=F32).astype(BF16)
    for hc in range(n_hc):
        st_ref[hc] = lax.dot_general(sk_ref[hc % 2], qb[:, hc * LANES:(hc + 1) * LANES],
                                     (((1,), (1,)), ((), ())), preferred_element_type=F32)


def _peerq(x, g, wq_bf, sk_bf):
    t, d = x.shape
    n_hc = wq_bf.shape[1] // LANES
    tm = _row_tile(t, 512)
    return pl.pallas_call(
        functools.partial(_peerq_kernel, n_hc=n_hc),
        grid=(t // tm,),
        in_specs=[pl.BlockSpec((tm, d), lambda i: (i, 0)), _full((1, d)), _full(wq_bf.shape),
                  _full(sk_bf.shape)],
        out_specs=[pl.BlockSpec((tm, d), lambda i: (i, 0)),
                   pl.BlockSpec((n_hc, N_KEYS, tm), lambda i: (0, 0, i))],
        out_shape=[jax.ShapeDtypeStruct((t, d), BF16), jax.ShapeDtypeStruct((n_hc, N_KEYS, t), F32)],
        compiler_params=_cparams("parallel"),
        name="peer_query",
    )(x, g, wq_bf, sk_bf)


def _batcher_network(n):
    pairs = []

    def merge(lo, m, r):
        step = 2 * r
        if step < m:
            merge(lo, m, step)
            merge(lo + r, m, step)
            pairs.extend((i, i + r) for i in range(lo + r, lo + m - r, step))
        else:
            pairs.append((lo, lo + r))

    def sort(lo, m):
        if m > 1:
            sort(lo, m // 2)
            sort(lo + m // 2, m // 2)
            merge(lo, m, 1)

    sort(0, n)
    return pairs


SUBLANES = 8
KEY_SLABS = N_KEYS // SUBLANES
SLAB_NETWORK = _batcher_network(KEY_SLABS)


def _pop_heads(lists, extra_heads, n):
    lists, extra_heads = list(lists), list(extra_heads)
    tops = []
    for r in range(n):
        head = lists[0]
        for e in extra_heads:
            head = jnp.maximum(head, e)
        m = jnp.max(head, axis=0, keepdims=True)
        tops.append(m)
        hit = lists[0] == m
        for d in range(min(n - 1 - r, len(lists))):
            nxt = lists[d + 1] if d + 1 < len(lists) else NEG_INF
            lists[d] = jnp.where(hit, nxt, lists[d])
        if r < n - 1:
            extra_heads = [jnp.where(e == m, NEG_INF, e) for e in extra_heads]
    return tops


def _top_sorted(s, n):
    slabs = [s[SUBLANES * r:SUBLANES * (r + 1)] for r in range(KEY_SLABS)]
    for i, j in SLAB_NETWORK:
        slabs[i], slabs[j] = jnp.maximum(slabs[i], slabs[j]), jnp.minimum(slabs[i], slabs[j])
    return _pop_heads(slabs, [], n)


def _peer_topk_kernel(st_ref, p1_ref, p2_ref, th_ref, *, n_heads):
    k = PEER_TOPK
    n = k + 1
    n_pad = -(-n // SUBLANES) * SUBLANES

    def head(h, carry):
        s1 = st_ref[2 * h]
        s2 = st_ref[2 * h + 1]
        t1 = _top_sorted(s1, n)
        t2 = _top_sorted(s2, n)
        row = lax.broadcasted_iota(jnp.int32, (n_pad, s1.shape[1]), 0)
        v1 = jnp.full((n_pad, s1.shape[1]), NEG_INF, F32)
        for i in range(n):
            v1 = jnp.where(row == i, t1[i], v1)
        row8 = row[0:SUBLANES]
        lists = [v1[0:SUBLANES] + t2[0]]
        for j in range(1, n):
            lists.append(jnp.where(row8 < n // (j + 1), v1[0:SUBLANES] + t2[j], NEG_INF))
        singles = [v1[SUBLANES * b:SUBLANES * (b + 1)] + t2[0] for b in range(1, n_pad // SUBLANES)]
        c = _pop_heads(lists, singles, n)
        m = c[0]
        z = jnp.ones_like(m)
        for r in range(1, k):
            z = z + jnp.exp(c[r] - m)
        inv_z = 1.0 / z
        th_ref[pl.ds(h, 1), :] = jnp.exp(0.5 * (c[k - 1] + c[k]) - m) * inv_z
        p1_ref[h] = jnp.exp(s1 - t1[0]) * inv_z
        p2_ref[h] = jnp.exp(s2 - t2[0])
        return carry

    lax.fori_loop(0, n_heads, head, 0)


def _peer_topk(st):
    n_hc, n, t = st.shape
    n_heads = n_hc // 2
    tk = LANES
    fac = pl.BlockSpec((n_heads, n, tk), lambda i: (0, 0, i))
    return pl.pallas_call(
        functools.partial(_peer_topk_kernel, n_heads=n_heads),
        grid=(t // tk,),
        in_specs=[pl.BlockSpec((n_hc, n, tk), lambda i: (0, 0, i))],
        out_specs=[fac, fac, pl.BlockSpec((n_heads, tk), lambda i: (0, i))],
        out_shape=[jax.ShapeDtypeStruct((n_heads, n, t), F32)] * 2 + [jax.ShapeDtypeStruct((n_heads, t), F32)],
        compiler_params=_cparams("parallel"),
        name="peer_topk",
    )(st)


ROW_BLK = 32
KEYS_PER_TILE = 8


def _peer_dense_kernel(xn_ref, x_ref, u_ref, v_ref, p1_ref, p2_ref, th_ref, o_ref, ht_ref, at_ref, acc_ref,
                       *, n_heads, tm):
    j = pl.program_id(1)

    @pl.when(j == 0)
    def _():
        acc_ref[...] = jnp.zeros_like(acc_ref)

    ht_ref[...] = lax.dot_general(u_ref[...], xn_ref[...], (((1,), (1,)), ((), ())),
                                  preferred_element_type=F32)

    for k in range(KEYS_PER_TILE):
        for lc in range(tm // LANES):
            cols = pl.ds(lc * LANES, LANES)
            p1_rows = [p1_ref[h, k:k + 1, cols] for h in range(n_heads)]
            th_rows = [th_ref[h:h + 1, cols] for h in range(n_heads)]
            for rb in range(N_KEYS // ROW_BLK):
                r2 = pl.ds(rb * ROW_BLK, ROW_BLK)
                r0 = pl.ds(k * N_KEYS + rb * ROW_BLK, ROW_BLK)
                gsum = jnp.zeros((ROW_BLK, LANES), F32)
                for h in range(n_heads):
                    w = p1_rows[h] * p2_ref[h, r2, cols]
                    gsum = gsum + jnp.where(w >= th_rows[h], w, 0.0)
                at_ref[r0, cols] = (_gelu(ht_ref[r0, cols]) * gsum).astype(BF16)

    acc_ref[...] += lax.dot_general(at_ref[...], v_ref[...], (((0,), (0,)), ((), ())),
                                    preferred_element_type=F32)

    @pl.when(j == pl.num_programs(1) - 1)
    def _():
        o_ref[...] = x_ref[...] + acc_ref[...]


def _peer_dense(xn, x, u_bf, v_bf, p1, p2, th):
    t, d = x.shape
    n_exp = u_bf.shape[0]
    n_heads = p1.shape[0]
    tm = _row_tile(t, 512)
    te = KEYS_PER_TILE * N_KEYS
    assert n_exp == N_KEYS * N_KEYS
    return pl.pallas_call(
        functools.partial(_peer_dense_kernel, n_heads=n_heads, tm=tm),
        grid=(t // tm, n_exp // te),
        in_specs=[pl.BlockSpec((tm, d), lambda i, j: (i, 0)),
                  pl.BlockSpec((tm, d), lambda i, j: (i, 0)),
                  pl.BlockSpec((te, d), lambda i, j: (j, 0)),
                  pl.BlockSpec((te, d), lambda i, j: (j, 0)),
                  pl.BlockSpec((n_heads, KEYS_PER_TILE, tm), lambda i, j: (0, j, i)),
                  pl.BlockSpec((n_heads, N_KEYS, tm), lambda i, j: (0, 0, i)),
                  pl.BlockSpec((n_heads, tm), lambda i, j: (0, i))],
        out_specs=pl.BlockSpec((tm, d), lambda i, j: (i, 0)),
        out_shape=jax.ShapeDtypeStruct((t, d), F32),
        scratch_shapes=[pltpu.VMEM((te, tm), F32), pltpu.VMEM((te, tm), BF16), pltpu.VMEM((tm, d), F32)],
        compiler_params=_cparams("parallel", "arbitrary"),
        name="peer_dense",
    )(xn, x, u_bf, v_bf, p1, p2, th)


def _ple_kernel(x_ref, p_ref, g_ref, gw_ref, pw_ref, gf_ref, o_ref, *, final):
    x = x_ref[...]
    hn = _rmsnorm(x, g_ref[...])
    gate = jax.nn.sigmoid(jnp.dot(hn.astype(BF16), gw_ref[...], preferred_element_type=F32))
    e = jnp.dot(p_ref[...].astype(BF16), pw_ref[...], preferred_element_type=F32)
    x = x + e * gate
    o_ref[...] = _rmsnorm(x, gf_ref[...]) if final else x


def _ple(x, p, g, gw_bf, pw_bf, gf, final):
    t, d = x.shape
    dp = p.shape[1]
    tm = _row_tile(t, 512)
    return pl.pallas_call(
        functools.partial(_ple_kernel, final=final),
        grid=(t // tm,),
        in_specs=[pl.BlockSpec((tm, d), lambda i: (i, 0)), pl.BlockSpec((tm, dp), lambda i: (i, 0)),
                  _full((1, d)), _full(gw_bf.shape), _full(pw_bf.shape), _full((1, d))],
        out_specs=pl.BlockSpec((tm, d), lambda i: (i, 0)),
        out_shape=jax.ShapeDtypeStruct((t, d), F32),
        compiler_params=_cparams("parallel"),
        name="ple",
    )(x, p, g, gw_bf, pw_bf, gf)


def _rwkv_prompt_scan(parts, batch, seq, n_heads):
    r, k, v, w, a, b = (x.reshape(batch, seq, n_heads * HEAD) for x in parts)
    y, s_fin = _scan_tokens(w, a, b, k, r, v)
    s_fin = s_fin.reshape(HEAD // 2, HEAD, 2, n_heads, batch).transpose(4, 3, 2, 0, 1)
    return y.reshape(batch * seq, n_heads * HEAD), s_fin.reshape(batch, n_heads, HEAD, HEAD)


def _rwkv_sample_scan(parts, wkv_prev, batch, n_heads):
    bh = batch * n_heads
    assert bh % LANES == 0
    rows = lambda x: x.reshape(batch, n_heads, HEAD).transpose(2, 0, 1).reshape(1, HEAD, bh)
    r, k, v, w, a, b = parts
    s0 = wkv_prev.transpose(2, 3, 0, 1).reshape(HEAD, HEAD, bh)
    y, s_fin = _scan(rows(w), rows(a), rows(b), rows(k), rows(r), rows(v), s0)
    y = y.reshape(HEAD, batch, n_heads).transpose(1, 2, 0).reshape(batch, n_heads * HEAD)
    return y, s_fin.reshape(HEAD, HEAD, batch, n_heads).transpose(2, 3, 0, 1)


def _layer(x, p, wkv_prev, shift_prev, lw, gf, final):
    batch, seq, d = x.shape
    t = batch * seq
    d_a = lw["w0"].shape[1]
    n_heads = d_a // HEAD
    p_a = lw["mu"].shape[1]
    xt = x.reshape(t, d)

    pa, pc = _proj(xt, lw["norm_mix_g"], lw["w_in"], p_a)
    pa3 = pa.reshape(batch, seq, p_a)
    r, k, v, w, a, b, g, bonus = _prep(pa, shift_prev, seq, lw["mu"], lw["w0"], lw["a0"], lw["wa2"], lw["g2"],
                                       lw["k_k"], lw["k_a"], lw["r_k"], lw["bd"])
    if wkv_prev is None:
        y, wkv_new = _rwkv_prompt_scan((r, k, v, w, a, b), batch, seq, n_heads)
        yc = _cmlp(pc, lw["cmlp_ln_g"], lw["cmlp_ln_b"], lw["cmlp_ws"], lw["cmlp_bs_rows"], lw["bd"])
        v_rows = None
    else:
        assert seq == 1
        y, wkv_new = _rwkv_sample_scan((r, k, v, w, a, b), wkv_prev, batch, n_heads)
        yc, v_rows = _cmlp_first(pc, lw["cmlp_ln_g"], lw["cmlp_ln_b"], lw["cmlp_w00"], lw["cmlp_b0"], lw["bd"])
        v_rows = v_rows.reshape(batch, seq, -1)
    x1 = _mix(xt, y, bonus, g, yc, lw["rwkv_ln_g"], lw["rwkv_ln_b"], lw["bd"], lw["w_out"])

    xn, st = _peerq(x1, lw["norm_ffn_g"], lw["peer_wq"], lw["peer_subkeys"])
    p1, p2, th = _peer_topk(st)
    x2 = _peer_dense(xn, x1, lw["peer_u"], lw["peer_v"], p1, p2, th)

    x3 = _ple(x2, p.reshape(t, -1), lw["norm_ple_g"], lw["ple_gate_w"], lw["ple_w"], gf, final)
    return x3.reshape(batch, seq, d), wkv_new, pa3[:, -1], v_rows


def kernel(x_prompt, x_sample, state_wkv, state_shift, p_prompt, p_sample, norm_mix_g, w_in, shift_mu, rwkv_w0, rwkv_w2, rwkv_a0, rwkv_a2, rwkv_g2, rwkv_k_k, rwkv_k_a, rwkv_r_k, rwkv_ln_g, rwkv_ln_b, cmlp_ln_g, cmlp_ln_b, cmlp_ws, cmlp_bs, w_out, norm_ffn_g, peer_wq, peer_subkeys, peer_u, peer_v, norm_ple_g, ple_w, ple_gate_w, norm_final_g):
    depth = state_wkv.shape[0]
    batch = x_prompt.shape[0]
    d_a = rwkv_w0.shape[1]
    d_c = cmlp_ln_g.shape[1]
    p_a = shift_mu.shape[1]
    lora_w = rwkv_w2.shape[1]
    lora_a = rwkv_a2.shape[1]
    assert lora_w == HEAD and lora_a == HEAD and d_a == d_c
    gidx = jnp.arange(d_a) // HEAD
    bd = (gidx[:, None] == gidx[None, :]).astype(BF16)
    row = lambda z: z.reshape(1, -1)
    gf = row(norm_final_g)

    hp, hs = x_prompt, x_sample
    outs = [[] for _ in range(5)]
    for i in range(depth):
        wa2 = jnp.zeros((lora_w + lora_a, 2 * d_a), F32)
        wa2 = wa2.at[:lora_w, :d_a].set(rwkv_w2[i]).at[lora_w:, d_a:].set(rwkv_a2[i])
        lw = dict(
            norm_mix_g=row(norm_mix_g[i]), w_in=w_in[i].astype(BF16), mu=row(shift_mu[i]),
            w0=row(rwkv_w0[i]), a0=row(rwkv_a0[i]), wa2=wa2.astype(BF16), g2=rwkv_g2[i].astype(BF16),
            k_k=row(rwkv_k_k[i]), k_a=row(rwkv_k_a[i]), r_k=row(rwkv_r_k[i]), bd=bd,
            rwkv_ln_g=row(rwkv_ln_g[i]), rwkv_ln_b=row(rwkv_ln_b[i]),
            cmlp_ln_g=row(cmlp_ln_g[i]), cmlp_ln_b=row(cmlp_ln_b[i]), cmlp_ws=cmlp_ws[i],
            cmlp_bs_rows=jnp.repeat(cmlp_bs[i].T, HEAD, axis=1),
            cmlp_w00=row(jnp.repeat(cmlp_ws[i][:, 0, 0], HEAD)), cmlp_b0=row(jnp.repeat(cmlp_bs[i][:, 0], HEAD)),
            w_out=w_out[i].astype(BF16), norm_ffn_g=row(norm_ffn_g[i]), peer_wq=peer_wq[i].astype(BF16),
            peer_subkeys=peer_subkeys[i].astype(BF16), peer_u=peer_u[i].astype(BF16),
            peer_v=peer_v[i].astype(BF16), norm_ple_g=row(norm_ple_g[i]),
            ple_w=ple_w[i].astype(BF16), ple_gate_w=ple_gate_w[i].astype(BF16),
        )
        final = i == depth - 1
        shift0 = jnp.zeros((batch, p_a), F32)
        hp, wkv_p, shift_p, _ = _layer(hp, p_prompt[i], None, shift0, lw, gf, final)
        hs, wkv_s, shift_s, v_s = _layer(hs, p_sample[i], state_wkv[i], state_shift[i], lw, gf, final)
        for o, val in zip(outs, (wkv_p, shift_p, wkv_s, shift_s, v_s)):
            o.append(val)
    return (hp, hs) + tuple(jnp.stack(o) for o in outs)
```

```python
import functools
import math

import jax
import jax.numpy as jnp
from jax import lax
from jax.experimental import pallas as pl
from jax.experimental.pallas import tpu as pltpu

F32 = jnp.float32
BF16 = jnp.bfloat16

LANES = 128
HEAD = 64
CHUNK = 128
N_KEYS = 128
PEER_TOPK = 16
RMS_EPS = 1e-6
LN_EPS = 1e-5
GN_EPS = 64e-5
SQRT_HALF = math.sqrt(0.5)
NEG_INF = float("-inf")
VMEM_LIMIT = 56 * 1024 * 1024


def _cparams(*sem):
    return pltpu.CompilerParams(dimension_semantics=sem, vmem_limit_bytes=VMEM_LIMIT)


def _rmsnorm(x, g):
    return x * lax.rsqrt(jnp.mean(x * x, axis=-1, keepdims=True) + RMS_EPS) * g


def _gelu(x):
    return 0.5 * x * (1.0 + lax.erf(x * SQRT_HALF))


def _group_sum(x, bd):
    hi = x.astype(BF16)
    rest = x - hi.astype(F32)
    mid = rest.astype(BF16)
    lo = (rest - mid.astype(F32)).astype(BF16)
    return (jnp.dot(hi, bd, preferred_element_type=F32) + jnp.dot(mid, bd, preferred_element_type=F32)
            + jnp.dot(lo, bd, preferred_element_type=F32))


def _row_tile(n, want):
    t = min(n, want)
    assert n % t == 0, (n, t)
    return t


def _full(shape):
    return pl.BlockSpec(shape, lambda *_: (0,) * len(shape))


def _proj_kernel(x_ref, g_ref, w_ref, pa_ref, pc_ref, *, p_a):
    h = _rmsnorm(x_ref[...], g_ref[...])
    p = jnp.dot(h.astype(BF16), w_ref[...], preferred_element_type=F32)
    pa_ref[...] = p[:, :p_a]
    pc_ref[...] = p[:, p_a:]


def _proj(x, g, w_bf, p_a):
    t, d = x.shape
    p_tot = w_bf.shape[1]
    tm = _row_tile(t, 512)
    return pl.pallas_call(
        functools.partial(_proj_kernel, p_a=p_a),
        grid=(t // tm,),
        in_specs=[pl.BlockSpec((tm, d), lambda i: (i, 0)), _full((1, d)), _full((d, p_tot))],
        out_specs=[pl.BlockSpec((tm, p_a), lambda i: (i, 0)),
                   pl.BlockSpec((tm, p_tot - p_a), lambda i: (i, 0))],
        out_shape=[jax.ShapeDtypeStruct((t, p_a), F32), jax.ShapeDtypeStruct((t, p_tot - p_a), F32)],
        compiler_params=_cparams("parallel"),
        name="proj",
    )(x, g, w_bf)


def _prep_kernel(pa_ref, prev_ref, mu_ref, w0_ref, a0_ref, wa2_ref, g2_ref, kk_ref, ka_ref, rk_ref, bd_ref,
                 r_o, k_o, v_o, w_o, a_o, b_o, g_o, bonus_o, *carry, d_a):
    pa = pa_ref[...]
    if carry:
        (carry_ref,) = carry
        first = jnp.where(pl.program_id(1) == 0, prev_ref[0], carry_ref[...])
        row = lax.broadcasted_iota(jnp.int32, pa.shape, 0)
        prev = jnp.where(row == 0, first, pltpu.roll(pa, 1, 0))
        carry_ref[...] = pa[pa.shape[0] - 1:, :]
    else:
        prev = prev_ref[...]
    xm = pa + (prev - pa) * mu_ref[...]
    r = xm[:, 0:d_a]
    k = xm[:, d_a:2 * d_a]
    v = xm[:, 2 * d_a:3 * d_a]
    wal = xm[:, 3 * d_a:3 * d_a + LANES]
    gl = xm[:, 3 * d_a + LANES:]
    lane = lax.broadcasted_iota(jnp.int32, wal.shape, 1)
    wal = jnp.where(lane < HEAD, jnp.tanh(wal), wal)
    lo = jnp.dot(wal.astype(BF16), wa2_ref[...], preferred_element_type=F32)
    z = -(w0_ref[...] + lo[:, :d_a])
    softplus = jnp.maximum(z, 0.0) + jnp.log1p(jnp.exp(-jnp.abs(z)))
    decay = jnp.exp(-jnp.exp(-softplus - 0.5))
    a = jax.nn.sigmoid(a0_ref[...] + lo[:, d_a:])
    g = jnp.dot(jax.nn.sigmoid(gl).astype(BF16), g2_ref[...], preferred_element_type=F32)
    bd = bd_ref[...]
    kk = k * kk_ref[...]
    kk = kk / jnp.maximum(jnp.sqrt(_group_sum(kk * kk, bd)), 1e-12)
    kh = k * (1.0 + (a - 1.0) * ka_ref[...])
    r_o[...] = r
    k_o[...] = kh
    v_o[...] = v
    w_o[...] = decay
    a_o[...] = -kk
    b_o[...] = kk * a
    g_o[...] = g
    bonus_o[...] = _group_sum(r * kh * rk_ref[...], bd) * v


def _prep(pa, shift_prev, seq, mu, w0, a0, wa2_bf, g2_bf, k_k, k_a, r_k, bd):
    t, p_a = pa.shape
    batch = t // seq
    d_a = w0.shape[1]
    if seq == 1:
        tm = _row_tile(t, 256)
        grid = (t // tm, 1)
        prev, prev_spec, scratch = shift_prev, pl.BlockSpec((tm, p_a), lambda i, l: (i, 0)), []
    else:
        tm = _row_tile(seq, 256)
        grid = (batch, seq // tm)
        prev = shift_prev.reshape(batch, 1, p_a)
        prev_spec, scratch = pl.BlockSpec((1, 1, p_a), lambda b, l: (b, 0, 0)), [pltpu.VMEM((1, p_a), F32)]
    n_l = grid[1]
    row = lambda n: pl.BlockSpec((tm, n), lambda b, l: (b * n_l + l, 0))
    const = lambda shape: pl.BlockSpec(shape, lambda b, l: (0,) * len(shape))
    return pl.pallas_call(
        functools.partial(_prep_kernel, d_a=d_a),
        grid=grid,
        in_specs=[row(p_a), prev_spec, const((1, p_a)), const((1, d_a)), const((1, d_a)),
                  const(wa2_bf.shape), const(g2_bf.shape), const((1, d_a)), const((1, d_a)),
                  const((1, d_a)), const(bd.shape)],
        out_specs=[row(d_a)] * 8,
        out_shape=[jax.ShapeDtypeStruct((t, d_a), F32)] * 8,
        scratch_shapes=scratch,
        compiler_params=_cparams("parallel", "arbitrary"),
        name="rwkv_prep",
    )(pa, prev, mu, w0, a0, wa2_bf, g2_bf, k_k, k_a, r_k, bd)


def _recurrence(st_ref, w_ref, a_ref, b_ref, k_ref, r_ref, v_ref, y_ref, tl, nv):
    def step(t, carry):
        for i in range(nv):
            s = st_ref[i]
            sa = jnp.sum(s * a_ref[t], axis=0, keepdims=True)
            s = s * w_ref[t] + sa * b_ref[t] + v_ref[t, pl.ds(i, 1), :] * k_ref[t]
            st_ref[i] = s
            y_ref[t, pl.ds(i, 1), :] = jnp.sum(s * r_ref[t], axis=0, keepdims=True)
        return carry

    lax.fori_loop(0, tl, step, 0)


def _scan_kernel(w_ref, a_ref, b_ref, k_ref, r_ref, v_ref, s0_ref, y_ref, sfin_ref, st_ref, *, tl, nv):
    l = pl.program_id(1)

    @pl.when(l == 0)
    def _():
        st_ref[...] = s0_ref[...]

    _recurrence(st_ref, w_ref, a_ref, b_ref, k_ref, r_ref, v_ref, y_ref, tl, nv)

    @pl.when(l == pl.num_programs(1) - 1)
    def _():
        sfin_ref[...] = st_ref[...]


def _scan(w, a, b, k, r, v, s0):
    l, n, gl = w.shape
    nv = v.shape[1]
    tl = _row_tile(l, 64)
    rows = pl.BlockSpec((tl, n, LANES), lambda g, i: (i, 0, g))
    vals = pl.BlockSpec((tl, nv, LANES), lambda g, i: (i, 0, g))
    state = pl.BlockSpec((nv, n, LANES), lambda g, i: (0, 0, g))
    return pl.pallas_call(
        functools.partial(_scan_kernel, tl=tl, nv=nv),
        grid=(gl // LANES, l // tl),
        in_specs=[rows] * 5 + [vals, state],
        out_specs=[vals, state],
        out_shape=[jax.ShapeDtypeStruct((l, nv, gl), F32), jax.ShapeDtypeStruct((nv, n, gl), F32)],
        scratch_shapes=[pltpu.VMEM((nv, n, LANES), F32)],
        compiler_params=_cparams("parallel", "arbitrary"),
        name="rwkv_scan",
    )(w, a, b, k, r, v, s0)


def _keys_to_lanes(x, n_heads):
    pieces = [x[:, h * HEAD:(h + 1) * HEAD] for h in range(n_heads)]
    return jnp.concatenate(pieces + pieces, axis=0).T


def _values_to_lanes(x, n_heads):
    half = HEAD // 2
    lo = [x[:, h * HEAD:h * HEAD + half] for h in range(n_heads)]
    hi = [x[:, h * HEAD + half:(h + 1) * HEAD] for h in range(n_heads)]
    return jnp.concatenate(lo + hi, axis=0).T


def _values_from_lanes(y, n_heads, batch):
    yt = y.T
    hb = n_heads * batch
    full = jnp.concatenate([yt[0:hb], yt[hb:2 * hb]], axis=1)
    return jnp.concatenate([full[h * batch:(h + 1) * batch] for h in range(n_heads)], axis=1)


N_KEY_ARRAYS = 5
N_SLOTS = 8
LOOKAHEAD = 2


def _scan_tokens_kernel(*refs, tl, nv, n_heads, batch):
    key_refs, v_ref = refs[:N_KEY_ARRAYS], refs[N_KEY_ARRAYS]
    y_ref, sfin_ref, st_ref, sa_ref = refs[N_KEY_ARRAYS + 1:N_KEY_ARRAYS + 5]
    slot_refs = refs[N_KEY_ARRAYS + 5:]
    n_slot = N_KEY_ARRAYS + 2
    slots = [slot_refs[n_slot * q:n_slot * (q + 1)] for q in range(N_SLOTS)]
    l = pl.program_id(0)

    @pl.when(l == 0)
    def _():
        st_ref[...] = jnp.zeros_like(st_ref)
        slots[-1][-1][...] = jnp.zeros_like(slots[-1][-1])

    def fill(slot, t):
        for src, dst in zip(key_refs, slot[:N_KEY_ARRAYS]):
            dst[...] = _keys_to_lanes(src[:, t, :], n_heads)
        slot[N_KEY_ARRAYS][...] = _values_to_lanes(v_ref[:, t, :], n_heads)

    for q in range(LOOKAHEAD):
        fill(slots[q], q)

    def one_step(cur, ahead, behind, t_ahead, t_behind):
        wt, at, bt, kt, rt, vt, yt = cur

        def relayout(job):
            if job == 0:
                y_ref[:, t_behind, :] = _values_from_lanes(behind[-1][...], n_heads, batch)
            elif job <= N_KEY_ARRAYS:
                ahead[job - 1][...] = _keys_to_lanes(key_refs[job - 1][:, t_ahead, :], n_heads)
            elif job == N_KEY_ARRAYS + 1:
                ahead[N_KEY_ARRAYS][...] = _values_to_lanes(v_ref[:, t_ahead, :], n_heads)

        for i in range(nv):
            sa_ref[pl.ds(i, 1), :] = jnp.sum(st_ref[i] * at[...], axis=0, keepdims=True)
            if i < n_slot:
                relayout(i)
        for i in range(nv):
            s = st_ref[i] * wt[...] + sa_ref[pl.ds(i, 1), :] * bt[...] + vt[pl.ds(i, 1), :] * kt[...]
            st_ref[i] = s
            yt[pl.ds(i, 1), :] = jnp.sum(s * rt[...], axis=0, keepdims=True)

    def steps(j, carry):
        for q in range(N_SLOTS):
            t = N_SLOTS * j + q
            one_step(slots[q], slots[(q + LOOKAHEAD) % N_SLOTS], slots[(q - 1) % N_SLOTS],
                     jnp.minimum(t + LOOKAHEAD, tl - 1), jnp.maximum(t - 1, 0))
        return carry

    lax.fori_loop(0, tl // N_SLOTS, steps, 0)
    y_ref[:, tl - 1, :] = _values_from_lanes(slots[-1][-1][...], n_heads, batch)

    @pl.when(l == pl.num_programs(0) - 1)
    def _():
        sfin_ref[...] = st_ref[...]


def _scan_tokens(w, a, b, k, r, v):
    batch, l, d = w.shape
    n_heads = d // HEAD
    assert 2 * batch * n_heads == LANES
    nv = HEAD // 2
    tl = _row_tile(l, 64)
    tok = pl.BlockSpec((batch, tl, d), lambda i: (0, i, 0))
    state = pl.BlockSpec((nv, HEAD, LANES), lambda i: (0, 0, 0))
    assert tl % N_SLOTS == 0 and LOOKAHEAD < N_SLOTS - 1
    slot = [pltpu.VMEM((HEAD, LANES), F32)] * N_KEY_ARRAYS + [pltpu.VMEM((nv, LANES), F32)] * 2
    return pl.pallas_call(
        functools.partial(_scan_tokens_kernel, tl=tl, nv=nv, n_heads=n_heads, batch=batch),
        grid=(l // tl,),
        in_specs=[tok] * 6,
        out_specs=[tok, state],
        out_shape=[jax.ShapeDtypeStruct((batch, l, d), F32), jax.ShapeDtypeStruct((nv, HEAD, LANES), F32)],
        scratch_shapes=[pltpu.VMEM((nv, HEAD, LANES), F32), pltpu.VMEM((nv, LANES), F32)] + slot * N_SLOTS,
        compiler_params=_cparams("arbitrary"),
        name="rwkv_scan_tokens",
    )(w, a, b, k, r, v)


def _cmlp_norm(pc, lng, lnb, bd, d_c):
    ge = _gelu(pc)
    u = ge[:, :d_c]
    v = ge[:, d_c:]
    mean = _group_sum(v, bd) * (1.0 / HEAD)
    d = v - mean
    var = _group_sum(d * d, bd) * (1.0 / HEAD)
    return u, d * lax.rsqrt(var + LN_EPS) * lng + lnb


def _cmlp_kernel(pc_ref, lng_ref, lnb_ref, ws_ref, bs_ref, bd_ref, yc_ref, *, d_c, n_chunks):
    u, vn = _cmlp_norm(pc_ref[...], lng_ref[...], lnb_ref[...], bd_ref[...], d_c)
    vb = vn.astype(BF16)
    ri = lax.broadcasted_iota(jnp.int32, (CHUNK, CHUNK), 0)
    ci = lax.broadcasted_iota(jnp.int32, (CHUNK, CHUNK), 1)
    lane = lax.broadcasted_iota(jnp.int32, (CHUNK, LANES), 1)
    n_groups = d_c // HEAD
    wsm = [jnp.where(ri >= ci, ws_ref[g], 0.0).astype(BF16) for g in range(n_groups)]
    for c in range(n_chunks):
        rows = slice(c * CHUNK, (c + 1) * CHUNK)
        for p in range(n_groups // 2):
            cols = slice(p * LANES, (p + 1) * LANES)
            vp = vb[rows, cols]
            s0 = jnp.dot(wsm[2 * p], vp, preferred_element_type=F32)
            s1 = jnp.dot(wsm[2 * p + 1], vp, preferred_element_type=F32)
            s = jnp.where(lane < HEAD, s0, s1) + bs_ref[:, cols]
            yc_ref[rows, cols] = u[rows, cols] * s


def _cmlp(pc, lng, lnb, ws, bs_rows, bd):
    t, p_c = pc.shape
    d_c = p_c // 2
    tm = _row_tile(t, 256)
    assert tm % CHUNK == 0
    return pl.pallas_call(
        functools.partial(_cmlp_kernel, d_c=d_c, n_chunks=tm // CHUNK),
        grid=(t // tm,),
        in_specs=[pl.BlockSpec((tm, p_c), lambda i: (i, 0)), _full((1, d_c)), _full((1, d_c)),
                  _full(ws.shape), _full(bs_rows.shape), _full(bd.shape)],
        out_specs=pl.BlockSpec((tm, d_c), lambda i: (i, 0)),
        out_shape=jax.ShapeDtypeStruct((t, d_c), F32),
        compiler_params=_cparams("parallel"),
        name="cmlp",
    )(pc, lng, lnb, ws, bs_rows, bd)


def _cmlp_first_kernel(pc_ref, lng_ref, lnb_ref, w00_ref, b0_ref, bd_ref, yc_ref, vn_ref, *, d_c):
    u, vn = _cmlp_norm(pc_ref[...], lng_ref[...], lnb_ref[...], bd_ref[...], d_c)
    yc_ref[...] = u * (w00_ref[...] * vn + b0_ref[...])
    vn_ref[...] = vn


def _cmlp_first(pc, lng, lnb, w00, b0, bd):
    t, p_c = pc.shape
    d_c = p_c // 2
    return pl.pallas_call(
        functools.partial(_cmlp_first_kernel, d_c=d_c),
        grid=(1,),
        in_specs=[_full(pc.shape), _full((1, d_c)), _full((1, d_c)), _full((1, d_c)), _full((1, d_c)),
                  _full(bd.shape)],
        out_specs=[_full((t, d_c))] * 2,
        out_shape=[jax.ShapeDtypeStruct((t, d_c), F32)] * 2,
        compiler_params=_cparams("arbitrary"),
        name="cmlp_first",
    )(pc, lng, lnb, w00, b0, bd)


def _mix_kernel(x_ref, y_ref, bonus_ref, g_ref, yc_ref, lng_ref, lnb_ref, bd_ref, wo_ref, o_ref, *, d_a):
    bd = bd_ref[...]
    y = y_ref[...]
    mean = _group_sum(y, bd) * (1.0 / HEAD)
    d = y - mean
    var = _group_sum(d * d, bd) * (1.0 / HEAD)
    yn = d * lax.rsqrt(var + GN_EPS) * lng_ref[...] + lnb_ref[...]
    ya = (yn + bonus_ref[...]) * g_ref[...]
    o_ref[...] = (x_ref[...]
                  + jnp.dot(ya.astype(BF16), wo_ref[0:d_a, :], preferred_element_type=F32)
                  + jnp.dot(yc_ref[...].astype(BF16), wo_ref[d_a:, :], preferred_element_type=F32))


def _mix(x, y, bonus, g, yc, lng, lnb, bd, wo_bf):
    t, d = x.shape
    d_a = y.shape[1]
    d_c = yc.shape[1]
    tm = _row_tile(t, 256)
    row = lambda n: pl.BlockSpec((tm, n), lambda i: (i, 0))
    return pl.pallas_call(
        functools.partial(_mix_kernel, d_a=d_a),
        grid=(t // tm,),
        in_specs=[row(d), row(d_a), row(d_a), row(d_a), row(d_c), _full((1, d_a)), _full((1, d_a)),
                  _full(bd.shape), _full(wo_bf.shape)],
        out_specs=row(d),
        out_shape=jax.ShapeDtypeStruct((t, d), F32),
        compiler_params=_cparams("parallel"),
        name="mix_out",
    )(x, y, bonus, g, yc, lng, lnb, bd, wo_bf)


def _peerq_kernel(x_ref, g_ref, wq_ref, sk_ref, xn_ref, st_ref, *, n_hc):
    hb = _rmsnorm(x_ref[...], g_ref[...]).astype(BF16)
    xn_ref[...] = hb
    qb = jnp.dot(hb, wq_ref[...], preferred_element_type=F32).astype(BF16)
    for hc in range(n_hc):
        st_ref[hc] = lax.dot_general(sk_ref[hc % 2], qb[:, hc * LANES:(hc + 1) * LANES],
                                     (((1,), (1,)), ((), ())), preferred_element_type=F32)


def _peerq(x, g, wq_bf, sk_bf):
    t, d = x.shape
    n_hc = wq_bf.shape[1] // LANES
    tm = _row_tile(t, 512)
    return pl.pallas_call(
        functools.partial(_peerq_kernel, n_hc=n_hc),
        grid=(t // tm,),
        in_specs=[pl.BlockSpec((tm, d), lambda i: (i, 0)), _full((1, d)), _full(wq_bf.shape),
                  _full(sk_bf.shape)],
        out_specs=[pl.BlockSpec((tm, d), lambda i: (i, 0)),
                   pl.BlockSpec((n_hc, N_KEYS, tm), lambda i: (0, 0, i))],
        out_shape=[jax.ShapeDtypeStruct((t, d), BF16), jax.ShapeDtypeStruct((n_hc, N_KEYS, t), F32)],
        compiler_params=_cparams("parallel"),
        name="peer_query",
    )(x, g, wq_bf, sk_bf)


def _batcher_network(n):
    pairs = []

    def merge(lo, m, r):
        step = 2 * r
        if step < m:
            merge(lo, m, step)
            merge(lo + r, m, step)
            pairs.extend((i, i + r) for i in range(lo + r, lo + m - r, step))
        else:
            pairs.append((lo, lo + r))

    def sort(lo, m):
        if m > 1:
            sort(lo, m // 2)
            sort(lo + m // 2, m // 2)
            merge(lo, m, 1)

    sort(0, n)
    return pairs


SUBLANES = 8
KEY_SLABS = N_KEYS // SUBLANES
SLAB_NETWORK = _batcher_network(KEY_SLABS)


def _pop_heads(lists, extra_heads, n):
    lists, extra_heads = list(lists), list(extra_heads)
    tops = []
    for r in range(n):
        head = lists[0]
        for e in extra_heads:
            head = jnp.maximum(head, e)
        m = jnp.max(head, axis=0, keepdims=True)
        tops.append(m)
        hit = lists[0] == m
        for d in range(min(n - 1 - r, len(lists))):
            nxt = lists[d + 1] if d + 1 < len(lists) else NEG_INF
            lists[d] = jnp.where(hit, nxt, lists[d])
        if r < n - 1:
            extra_heads = [jnp.where(e == m, NEG_INF, e) for e in extra_heads]
    return tops


def _top_sorted(s, n):
    slabs = [s[SUBLANES * r:SUBLANES * (r + 1)] for r in range(KEY_SLABS)]
    for i, j in SLAB_NETWORK:
        slabs[i], slabs[j] = jnp.maximum(slabs[i], slabs[j]), jnp.minimum(slabs[i], slabs[j])
    return _pop_heads(slabs, [], n)


def _peer_topk_kernel(st_ref, p1_ref, p2_ref, th_ref, *, n_heads):
    k = PEER_TOPK
    n = k + 1
    n_pad = -(-n // SUBLANES) * SUBLANES

    def head(h, carry):
        s1 = st_ref[2 * h]
        s2 = st_ref[2 * h + 1]
        t1 = _top_sorted(s1, n)
        t2 = _top_sorted(s2, n)
        row = lax.broadcasted_iota(jnp.int32, (n_pad, s1.shape[1]), 0)
        v1 = jnp.full((n_pad, s1.shape[1]), NEG_INF, F32)
        for i in range(n):
            v1 = jnp.where(row == i, t1[i], v1)
        row8 = row[0:SUBLANES]
        lists = [v1[0:SUBLANES] + t2[0]]
        for j in range(1, n):
            lists.append(jnp.where(row8 < n // (j + 1), v1[0:SUBLANES] + t2[j], NEG_INF))
        singles = [v1[SUBLANES * b:SUBLANES * (b + 1)] + t2[0] for b in range(1, n_pad // SUBLANES)]
        c = _pop_heads(lists, singles, n)
        m = c[0]
        z = jnp.ones_like(m)
        for r in range(1, k):
            z = z + jnp.exp(c[r] - m)
        inv_z = 1.0 / z
        th_ref[pl.ds(h, 1), :] = jnp.exp(0.5 * (c[k - 1] + c[k]) - m) * inv_z
        p1_ref[h] = jnp.exp(s1 - t1[0]) * inv_z
        p2_ref[h] = jnp.exp(s2 - t2[0])
        return carry

    lax.fori_loop(0, n_heads, head, 0)


def _peer_topk(st):
    n_hc, n, t = st.shape
    n_heads = n_hc // 2
    tk = LANES
    fac = pl.BlockSpec((n_heads, n, tk), lambda i: (0, 0, i))
    return pl.pallas_call(
        functools.partial(_peer_topk_kernel, n_heads=n_heads),
        grid=(t // tk,),
        in_specs=[pl.BlockSpec((n_hc, n, tk), lambda i: (0, 0, i))],
        out_specs=[fac, fac, pl.BlockSpec((n_heads, tk), lambda i: (0, i))],
        out_shape=[jax.ShapeDtypeStruct((n_heads, n, t), F32)] * 2 + [jax.ShapeDtypeStruct((n_heads, t), F32)],
        compiler_params=_cparams("parallel"),
        name="peer_topk",
    )(st)


ROW_BLK = 32
KEYS_PER_TILE = 8


def _peer_dense_kernel(xn_ref, x_ref, u_ref, v_ref, p1_ref, p2_ref, th_ref, o_ref, ht_ref, at_ref, acc_ref,
                       *, n_heads, tm):
    j = pl.program_id(1)

    @pl.when(j == 0)
    def _():
        acc_ref[...] = jnp.zeros_like(acc_ref)

    ht_ref[...] = lax.dot_general(u_ref[...], xn_ref[...], (((1,), (1,)), ((), ())),
                                  preferred_element_type=F32)

    for k in range(KEYS_PER_TILE):
        for lc in range(tm // LANES):
            cols = pl.ds(lc * LANES, LANES)
            p1_rows = [p1_ref[h, k:k + 1, cols] for h in range(n_heads)]
            th_rows = [th_ref[h:h + 1, cols] for h in range(n_heads)]
            for rb in range(N_KEYS // ROW_BLK):
                r2 = pl.ds(rb * ROW_BLK, ROW_BLK)
                r0 = pl.ds(k * N_KEYS + rb * ROW_BLK, ROW_BLK)
                gsum = jnp.zeros((ROW_BLK, LANES), F32)
                for h in range(n_heads):
                    w = p1_rows[h] * p2_ref[h, r2, cols]
                    gsum = gsum + jnp.where(w >= th_rows[h], w, 0.0)
                at_ref[r0, cols] = (_gelu(ht_ref[r0, cols]) * gsum).astype(BF16)

    acc_ref[...] += lax.dot_general(at_ref[...], v_ref[...], (((0,), (0,)), ((), ())),
                                    preferred_element_type=F32)

    @pl.when(j == pl.num_programs(1) - 1)
    def _():
        o_ref[...] = x_ref[...] + acc_ref[...]


def _peer_dense(xn, x, u_bf, v_bf, p1, p2, th):
    t, d = x.shape
    n_exp = u_bf.shape[0]
    n_heads = p1.shape[0]
    tm = _row_tile(t, 512)
    te = KEYS_PER_TILE * N_KEYS
    assert n_exp == N_KEYS * N_KEYS
    return pl.pallas_call(
        functools.partial(_peer_dense_kernel, n_heads=n_heads, tm=tm),
        grid=(t // tm, n_exp // te),
        in_specs=[pl.BlockSpec((tm, d), lambda i, j: (i, 0)),
                  pl.BlockSpec((tm, d), lambda i, j: (i, 0)),
                  pl.BlockSpec((te, d), lambda i, j: (j, 0)),
                  pl.BlockSpec((te, d), lambda i, j: (j, 0)),
                  pl.BlockSpec((n_heads, KEYS_PER_TILE, tm), lambda i, j: (0, j, i)),
                  pl.BlockSpec((n_heads, N_KEYS, tm), lambda i, j: (0, 0, i)),
                  pl.BlockSpec((n_heads, tm), lambda i, j: (0, i))],
        out_specs=pl.BlockSpec((tm, d), lambda i, j: (i, 0)),
        out_shape=jax.ShapeDtypeStruct((t, d), F32),
        scratch_shapes=[pltpu.VMEM((te, tm), F32), pltpu.VMEM((te, tm), BF16), pltpu.VMEM((tm, d), F32)],
        compiler_params=_cparams("parallel", "arbitrary"),
        name="peer_dense",
    )(xn, x, u_bf, v_bf, p1, p2, th)


def _ple_kernel(x_ref, p_ref, g_ref, gw_ref, pw_ref, gf_ref, o_ref, *, final):
    x = x_ref[...]
    hn = _rmsnorm(x, g_ref[...])
    gate = jax.nn.sigmoid(jnp.dot(hn.astype(BF16), gw_ref[...], preferred_element_type=F32))
    e = jnp.dot(p_ref[...].astype(BF16), pw_ref[...], preferred_element_type=F32)
    x = x + e * gate
    o_ref[...] = _rmsnorm(x, gf_ref[...]) if final else x


def _ple(x, p, g, gw_bf, pw_bf, gf, final):
    t, d = x.shape
    dp = p.shape[1]
    tm = _row_tile(t, 512)
    return pl.pallas_call(
        functools.partial(_ple_kernel, final=final),
        grid=(t // tm,),
        in_specs=[pl.BlockSpec((tm, d), lambda i: (i, 0)), pl.BlockSpec((tm, dp), lambda i: (i, 0)),
                  _full((1, d)), _full(gw_bf.shape), _full(pw_bf.shape), _full((1, d))],
        out_specs=pl.BlockSpec((tm, d), lambda i: (i, 0)),
        out_shape=jax.ShapeDtypeStruct((t, d), F32),
        compiler_params=_cparams("parallel"),
        name="ple",
    )(x, p, g, gw_bf, pw_bf, gf)


def _rwkv_prompt_scan(parts, batch, seq, n_heads):
    r, k, v, w, a, b = (x.reshape(batch, seq, n_heads * HEAD) for x in parts)
    y, s_fin = _scan_tokens(w, a, b, k, r, v)
    s_fin = s_fin.reshape(HEAD // 2, HEAD, 2, n_heads, batch).transpose(4, 3, 2, 0, 1)
    return y.reshape(batch * seq, n_heads * HEAD), s_fin.reshape(batch, n_heads, HEAD, HEAD)


def _rwkv_sample_scan(parts, wkv_prev, batch, n_heads):
    bh = batch * n_heads
    assert bh % LANES == 0
    rows = lambda x: x.reshape(batch, n_heads, HEAD).transpose(2, 0, 1).reshape(1, HEAD, bh)
    r, k, v, w, a, b = parts
    s0 = wkv_prev.transpose(2, 3, 0, 1).reshape(HEAD, HEAD, bh)
    y, s_fin = _scan(rows(w), rows(a), rows(b), rows(k), rows(r), rows(v), s0)
    y = y.reshape(HEAD, batch, n_heads).transpose(1, 2, 0).reshape(batch, n_heads * HEAD)
    return y, s_fin.reshape(HEAD, HEAD, batch, n_heads).transpose(2, 3, 0, 1)


def _layer(x, p, wkv_prev, shift_prev, lw, gf, final):
    batch, seq, d = x.shape
    t = batch * seq
    d_a = lw["w0"].shape[1]
    n_heads = d_a // HEAD
    p_a = lw["mu"].shape[1]
    xt = x.reshape(t, d)

    pa, pc = _proj(xt, lw["norm_mix_g"], lw["w_in"], p_a)
    pa3 = pa.reshape(batch, seq, p_a)
    r, k, v, w, a, b, g, bonus = _prep(pa, shift_prev, seq, lw["mu"], lw["w0"], lw["a0"], lw["wa2"], lw["g2"],
                                       lw["k_k"], lw["k_a"], lw["r_k"], lw["bd"])
    if wkv_prev is None:
        y, wkv_new = _rwkv_prompt_scan((r, k, v, w, a, b), batch, seq, n_heads)
        yc = _cmlp(pc, lw["cmlp_ln_g"], lw["cmlp_ln_b"], lw["cmlp_ws"], lw["cmlp_bs_rows"], lw["bd"])
        v_rows = None
    else:
        assert seq == 1
        y, wkv_new = _rwkv_sample_scan((r, k, v, w, a, b), wkv_prev, batch, n_heads)
        yc, v_rows = _cmlp_first(pc, lw["cmlp_ln_g"], lw["cmlp_ln_b"], lw["cmlp_w00"], lw["cmlp_b0"], lw["bd"])
        v_rows = v_rows.reshape(batch, seq, -1)
    x1 = _mix(xt, y, bonus, g, yc, lw["rwkv_ln_g"], lw["rwkv_ln_b"], lw["bd"], lw["w_out"])

    xn, st = _peerq(x1, lw["norm_ffn_g"], lw["peer_wq"], lw["peer_subkeys"])
    p1, p2, th = _peer_topk(st)
    x2 = _peer_dense(xn, x1, lw["peer_u"], lw["peer_v"], p1, p2, th)

    x3 = _ple(x2, p.reshape(t, -1), lw["norm_ple_g"], lw["ple_gate_w"], lw["ple_w"], gf, final)
    return x3.reshape(batch, seq, d), wkv_new, pa3[:, -1], v_rows


def kernel(x_prompt, x_sample, state_wkv, state_shift, p_prompt, p_sample, norm_mix_g, w_in, shift_mu, rwkv_w0, rwkv_w2, rwkv_a0, rwkv_a2, rwkv_g2, rwkv_k_k, rwkv_k_a, rwkv_r_k, rwkv_ln_g, rwkv_ln_b, cmlp_ln_g, cmlp_ln_b, cmlp_ws, cmlp_bs, w_out, norm_ffn_g, peer_wq, peer_subkeys, peer_u, peer_v, norm_ple_g, ple_w, ple_gate_w, norm_final_g):
    depth = state_wkv.shape[0]
    batch = x_prompt.shape[0]
    d_a = rwkv_w0.shape[1]
    d_c = cmlp_ln_g.shape[1]
    p_a = shift_mu.shape[1]
    lora_w = rwkv_w2.shape[1]
    lora_a = rwkv_a2.shape[1]
    assert lora_w == HEAD and lora_a == HEAD and d_a == d_c
    gidx = jnp.arange(d_a) // HEAD
    bd = (gidx[:, None] == gidx[None, :]).astype(BF16)
    row = lambda z: z.reshape(1, -1)
    gf = row(norm_final_g)

    hp, hs = x_prompt, x_sample
    outs = [[] for _ in range(5)]
    for i in range(depth):
        wa2 = jnp.zeros((lora_w + lora_a, 2 * d_a), F32)
        wa2 = wa2.at[:lora_w, :d_a].set(rwkv_w2[i]).at[lora_w:, d_a:].set(rwkv_a2[i])
        lw = dict(
            norm_mix_g=row(norm_mix_g[i]), w_in=w_in[i].astype(BF16), mu=row(shift_mu[i]),
            w0=row(rwkv_w0[i]), a0=row(rwkv_a0[i]), wa2=wa2.astype(BF16), g2=rwkv_g2[i].astype(BF16),
            k_k=row(rwkv_k_k[i]), k_a=row(rwkv_k_a[i]), r_k=row(rwkv_r_k[i]), bd=bd,
            rwkv_ln_g=row(rwkv_ln_g[i]), rwkv_ln_b=row(rwkv_ln_b[i]),
            cmlp_ln_g=row(cmlp_ln_g[i]), cmlp_ln_b=row(cmlp_ln_b[i]), cmlp_ws=cmlp_ws[i],
            cmlp_bs_rows=jnp.repeat(cmlp_bs[i].T, HEAD, axis=1),
            cmlp_w00=row(jnp.repeat(cmlp_ws[i][:, 0, 0], HEAD)), cmlp_b0=row(jnp.repeat(cmlp_bs[i][:, 0], HEAD)),
            w_out=w_out[i].astype(BF16), norm_ffn_g=row(norm_ffn_g[i]), peer_wq=peer_wq[i].astype(BF16),
            peer_subkeys=peer_subkeys[i].astype(BF16), peer_u=peer_u[i].astype(BF16),
            peer_v=peer_v[i].astype(BF16), norm_ple_g=row(norm_ple_g[i]),
            ple_w=ple_w[i].astype(BF16), ple_gate_w=ple_gate_w[i].astype(BF16),
        )
        final = i == depth - 1
        shift0 = jnp.zeros((batch, p_a), F32)
        hp, wkv_p, shift_p, _ = _layer(hp, p_prompt[i], None, shift0, lw, gf, final)
        hs, wkv_s, shift_s, v_s = _layer(hs, p_sample[i], state_wkv[i], state_shift[i], lw, gf, final)
        for o, val in zip(outs, (wkv_p, shift_p, wkv_s, shift_s, v_s)):
            o.append(val)
    return (hp, hs) + tuple(jnp.stack(o) for o in outs)
```

```python
import functools
import math

import jax
import jax.numpy as jnp
from jax import lax
from jax.experimental import pallas as pl
from jax.experimental.pallas import tpu as pltpu

F32 = jnp.float32
BF16 = jnp.bfloat16

LANES = 128
HEAD = 64
CHUNK = 128
N_KEYS = 128
PEER_TOPK = 16
RMS_EPS = 1e-6
LN_EPS = 1e-5
GN_EPS = 64e-5
SQRT_HALF = math.sqrt(0.5)
NEG_INF = float("-inf")
VMEM_LIMIT = 56 * 1024 * 1024


def _cparams(*sem):
    return pltpu.CompilerParams(dimension_semantics=sem, vmem_limit_bytes=VMEM_LIMIT)


def _rmsnorm(x, g):
    return x * lax.rsqrt(jnp.mean(x * x, axis=-1, keepdims=True) + RMS_EPS) * g


def _gelu(x):
    return 0.5 * x * (1.0 + lax.erf(x * SQRT_HALF))


def _group_sum(x, bd):
    hi = x.astype(BF16)
    rest = x - hi.astype(F32)
    mid = rest.astype(BF16)
    lo = (rest - mid.astype(F32)).astype(BF16)
    return (jnp.dot(hi, bd, preferred_element_type=F32) + jnp.dot(mid, bd, preferred_element_type=F32)
            + jnp.dot(lo, bd, preferred_element_type=F32))


def _row_tile(n, want):
    t = min(n, want)
    assert n % t == 0, (n, t)
    return t


def _full(shape):
    return pl.BlockSpec(shape, lambda *_: (0,) * len(shape))


def _proj_kernel(x_ref, g_ref, w_ref, pa_ref, pc_ref, *, p_a):
    h = _rmsnorm(x_ref[...], g_ref[...])
    p = jnp.dot(h.astype(BF16), w_ref[...], preferred_element_type=F32)
    pa_ref[...] = p[:, :p_a]
    pc_ref[...] = p[:, p_a:]


def _proj(x, g, w_bf, p_a):
    t, d = x.shape
    p_tot = w_bf.shape[1]
    tm = _row_tile(t, 512)
    return pl.pallas_call(
        functools.partial(_proj_kernel, p_a=p_a),
        grid=(t // tm,),
        in_specs=[pl.BlockSpec((tm, d), lambda i: (i, 0)), _full((1, d)), _full((d, p_tot))],
        out_specs=[pl.BlockSpec((tm, p_a), lambda i: (i, 0)),
                   pl.BlockSpec((tm, p_tot - p_a), lambda i: (i, 0))],
        out_shape=[jax.ShapeDtypeStruct((t, p_a), F32), jax.ShapeDtypeStruct((t, p_tot - p_a), F32)],
        compiler_params=_cparams("parallel"),
        name="proj",
    )(x, g, w_bf)


def _prep_kernel(pa_ref, prev_ref, mu_ref, w0_ref, a0_ref, wa2_ref, g2_ref, kk_ref, ka_ref, rk_ref, bd_ref,
                 r_o, k_o, v_o, w_o, a_o, b_o, g_o, bonus_o, *carry, d_a):
    pa = pa_ref[...]
    if carry:
        (carry_ref,) = carry
        first = jnp.where(pl.program_id(1) == 0, prev_ref[0], carry_ref[...])
        row = lax.broadcasted_iota(jnp.int32, pa.shape, 0)
        prev = jnp.where(row == 0, first, pltpu.roll(pa, 1, 0))
        carry_ref[...] = pa[pa.shape[0] - 1:, :]
    else:
        prev = prev_ref[...]
    xm = pa + (prev - pa) * mu_ref[...]
    r = xm[:, 0:d_a]
    k = xm[:, d_a:2 * d_a]
    v = xm[:, 2 * d_a:3 * d_a]
    wal = xm[:, 3 * d_a:3 * d_a + LANES]
    gl = xm[:, 3 * d_a + LANES:]
    lane = lax.broadcasted_iota(jnp.int32, wal.shape, 1)
    wal = jnp.where(lane < HEAD, jnp.tanh(wal), wal)
    lo = jnp.dot(wal.astype(BF16), wa2_ref[...], preferred_element_type=F32)
    z = -(w0_ref[...] + lo[:, :d_a])
    softplus = jnp.maximum(z, 0.0) + jnp.log1p(jnp.exp(-jnp.abs(z)))
    decay = jnp.exp(-jnp.exp(-softplus - 0.5))
    a = jax.nn.sigmoid(a0_ref[...] + lo[:, d_a:])
    g = jnp.dot(jax.nn.sigmoid(gl).astype(BF16), g2_ref[...], preferred_element_type=F32)
    bd = bd_ref[...]
    kk = k * kk_ref[...]
    kk = kk / jnp.maximum(jnp.sqrt(_group_sum(kk * kk, bd)), 1e-12)
    kh = k * (1.0 + (a - 1.0) * ka_ref[...])
    r_o[...] = r
    k_o[...] = kh
    v_o[...] = v
    w_o[...] = decay
    a_o[...] = -kk
    b_o[...] = kk * a
    g_o[...] = g
    bonus_o[...] = _group_sum(r * kh * rk_ref[...], bd) * v


def _prep(pa, shift_prev, seq, mu, w0, a0, wa2_bf, g2_bf, k_k, k_a, r_k, bd):
    t, p_a = pa.shape
    batch = t // seq
    d_a = w0.shape[1]
    if seq == 1:
        tm = _row_tile(t, 256)
        grid = (t // tm, 1)
        prev, prev_spec, scratch = shift_prev, pl.BlockSpec((tm, p_a), lambda i, l: (i, 0)), []
    else:
        tm = _row_tile(seq, 256)
        grid = (batch, seq // tm)
        prev = shift_prev.reshape(batch, 1, p_a)
        prev_spec, scratch = pl.BlockSpec((1, 1, p_a), lambda b, l: (b, 0, 0)), [pltpu.VMEM((1, p_a), F32)]
    n_l = grid[1]
    row = lambda n: pl.BlockSpec((tm, n), lambda b, l: (b * n_l + l, 0))
    const = lambda shape: pl.BlockSpec(shape, lambda b, l: (0,) * len(shape))
    return pl.pallas_call(
        functools.partial(_prep_kernel, d_a=d_a),
        grid=grid,
        in_specs=[row(p_a), prev_spec, const((1, p_a)), const((1, d_a)), const((1, d_a)),
                  const(wa2_bf.shape), const(g2_bf.shape), const((1, d_a)), const((1, d_a)),
                  const((1, d_a)), const(bd.shape)],
        out_specs=[row(d_a)] * 8,
        out_shape=[jax.ShapeDtypeStruct((t, d_a), F32)] * 8,
        scratch_shapes=scratch,
        compiler_params=_cparams("parallel", "arbitrary"),
        name="rwkv_prep",
    )(pa, prev, mu, w0, a0, wa2_bf, g2_bf, k_k, k_a, r_k, bd)


def _recurrence(st_ref, w_ref, a_ref, b_ref, k_ref, r_ref, v_ref, y_ref, tl, nv):
    def step(t, carry):
        for i in range(nv):
            s = st_ref[i]
            sa = jnp.sum(s * a_ref[t], axis=0, keepdims=True)
            s = s * w_ref[t] + sa * b_ref[t] + v_ref[t, pl.ds(i, 1), :] * k_ref[t]
            st_ref[i] = s
            y_ref[t, pl.ds(i, 1), :] = jnp.sum(s * r_ref[t], axis=0, keepdims=True)
        return carry

    lax.fori_loop(0, tl, step, 0)


def _scan_kernel(w_ref, a_ref, b_ref, k_ref, r_ref, v_ref, s0_ref, y_ref, sfin_ref, st_ref, *, tl, nv):
    l = pl.program_id(1)

    @pl.when(l == 0)
    def _():
        st_ref[...] = s0_ref[...]

    _recurrence(st_ref, w_ref, a_ref, b_ref, k_ref, r_ref, v_ref, y_ref, tl, nv)

    @pl.when(l == pl.num_programs(1) - 1)
    def _():
        sfin_ref[...] = st_ref[...]


def _scan(w, a, b, k, r, v, s0):
    l, n, gl = w.shape
    nv = v.shape[1]
    tl = _row_tile(l, 64)
    rows = pl.BlockSpec((tl, n, LANES), lambda g, i: (i, 0, g))
    vals = pl.BlockSpec((tl, nv, LANES), lambda g, i: (i, 0, g))
    state = pl.BlockSpec((nv, n, LANES), lambda g, i: (0, 0, g))
    return pl.pallas_call(
        functools.partial(_scan_kernel, tl=tl, nv=nv),
        grid=(gl // LANES, l // tl),
        in_specs=[rows] * 5 + [vals, state],
        out_specs=[vals, state],
        out_shape=[jax.ShapeDtypeStruct((l, nv, gl), F32), jax.ShapeDtypeStruct((nv, n, gl), F32)],
        scratch_shapes=[pltpu.VMEM((nv, n, LANES), F32)],
        compiler_params=_cparams("parallel", "arbitrary"),
        name="rwkv_scan",
    )(w, a, b, k, r, v, s0)


def _keys_to_lanes(x, n_heads):
    pieces = [x[:, h * HEAD:(h + 1) * HEAD] for h in range(n_heads)]
    return jnp.concatenate(pieces + pieces, axis=0).T


def _values_to_lanes(x, n_heads):
    half = HEAD // 2
    lo = [x[:, h * HEAD:h * HEAD + half] for h in range(n_heads)]
    hi = [x[:, h * HEAD + half:(h + 1) * HEAD] for h in range(n_heads)]
    return jnp.concatenate(lo + hi, axis=0).T


def _values_from_lanes(y, n_heads, batch):
    yt = y.T
    hb = n_heads * batch
    full = jnp.concatenate([yt[0:hb], yt[hb:2 * hb]], axis=1)
    return jnp.concatenate([full[h * batch:(h + 1) * batch] for h in range(n_heads)], axis=1)


N_KEY_ARRAYS = 5
N_SLOTS = 8
LOOKAHEAD = 2


def _scan_tokens_kernel(*refs, tl, nv, n_heads, batch):
    key_refs, v_ref = refs[:N_KEY_ARRAYS], refs[N_KEY_ARRAYS]
    y_ref, sfin_ref, st_ref, sa_ref = refs[N_KEY_ARRAYS + 1:N_KEY_ARRAYS + 5]
    slot_refs = refs[N_KEY_ARRAYS + 5:]
    n_slot = N_KEY_ARRAYS + 2
    slots = [slot_refs[n_slot * q:n_slot * (q + 1)] for q in range(N_SLOTS)]
    l = pl.program_id(0)

    @pl.when(l == 0)
    def _():
        st_ref[...] = jnp.zeros_like(st_ref)
        slots[-1][-1][...] = jnp.zeros_like(slots[-1][-1])

    def fill(slot, t):
        for src, dst in zip(key_refs, slot[:N_KEY_ARRAYS]):
            dst[...] = _keys_to_lanes(src[:, t, :], n_heads)
        slot[N_KEY_ARRAYS][...] = _values_to_lanes(v_ref[:, t, :], n_heads)

    for q in range(LOOKAHEAD):
        fill(slots[q], q)

    def one_step(cur, ahead, behind, t_ahead, t_behind):
        wt, at, bt, kt, rt, vt, yt = cur

        def relayout(job):
            if job == 0:
                y_ref[:, t_behind, :] = _values_from_lanes(behind[-1][...], n_heads, batch)
            elif job <= N_KEY_ARRAYS:
                ahead[job - 1][...] = _keys_to_lanes(key_refs[job - 1][:, t_ahead, :], n_heads)
            elif job == N_KEY_ARRAYS + 1:
                ahead[N_KEY_ARRAYS][...] = _values_to_lanes(v_ref[:, t_ahead, :], n_heads)

        for i in range(nv):
            sa_ref[pl.ds(i, 1), :] = jnp.sum(st_ref[i] * at[...], axis=0, keepdims=True)
            if i < n_slot:
                relayout(i)
        for i in range(nv):
            s = st_ref[i] * wt[...] + sa_ref[pl.ds(i, 1), :] * bt[...] + vt[pl.ds(i, 1), :] * kt[...]
            st_ref[i] = s
            yt[pl.ds(i, 1), :] = jnp.sum(s * rt[...], axis=0, keepdims=True)

    def steps(j, carry):
        for q in range(N_SLOTS):
            t = N_SLOTS * j + q
            one_step(slots[q], slots[(q + LOOKAHEAD) % N_SLOTS], slots[(q - 1) % N_SLOTS],
                     jnp.minimum(t + LOOKAHEAD, tl - 1), jnp.maximum(t - 1, 0))
        return carry

    lax.fori_loop(0, tl // N_SLOTS, steps, 0)
    y_ref[:, tl - 1, :] = _values_from_lanes(slots[-1][-1][...], n_heads, batch)

    @pl.when(l == pl.num_programs(0) - 1)
    def _():
        sfin_ref[...] = st_ref[...]


def _scan_tokens(w, a, b, k, r, v):
    batch, l, d = w.shape
    n_heads = d // HEAD
    assert 2 * batch * n_heads == LANES
    nv = HEAD // 2
    tl = _row_tile(l, 64)
    tok = pl.BlockSpec((batch, tl, d), lambda i: (0, i, 0))
    state = pl.BlockSpec((nv, HEAD, LANES), lambda i: (0, 0, 0))
    assert tl % N_SLOTS == 0 and LOOKAHEAD < N_SLOTS - 1
    slot = [pltpu.VMEM((HEAD, LANES), F32)] * N_KEY_ARRAYS + [pltpu.VMEM((nv, LANES), F32)] * 2
    return pl.pallas_call(
        functools.partial(_scan_tokens_kernel, tl=tl, nv=nv, n_heads=n_heads, batch=batch),
        grid=(l // tl,),
        in_specs=[tok] * 6,
        out_specs=[tok, state],
        out_shape=[jax.ShapeDtypeStruct((batch, l, d), F32), jax.ShapeDtypeStruct((nv, HEAD, LANES), F32)],
        scratch_shapes=[pltpu.VMEM((nv, HEAD, LANES), F32), pltpu.VMEM((nv, LANES), F32)] + slot * N_SLOTS,
        compiler_params=_cparams("arbitrary"),
        name="rwkv_scan_tokens",
    )(w, a, b, k, r, v)


def _cmlp_norm(pc, lng, lnb, bd, d_c):
    ge = _gelu(pc)
    u = ge[:, :d_c]
    v = ge[:, d_c:]
    mean = _group_sum(v, bd) * (1.0 / HEAD)
    d = v - mean
    var = _group_sum(d * d, bd) * (1.0 / HEAD)
    return u, d * lax.rsqrt(var + LN_EPS) * lng + lnb


def _cmlp_kernel(pc_ref, lng_ref, lnb_ref, ws_ref, bs_ref, bd_ref, yc_ref, *, d_c, n_chunks):
    u, vn = _cmlp_norm(pc_ref[...], lng_ref[...], lnb_ref[...], bd_ref[...], d_c)
    vb = vn.astype(BF16)
    ri = lax.broadcasted_iota(jnp.int32, (CHUNK, CHUNK), 0)
    ci = lax.broadcasted_iota(jnp.int32, (CHUNK, CHUNK), 1)
    lane = lax.broadcasted_iota(jnp.int32, (CHUNK, LANES), 1)
    n_groups = d_c // HEAD
    wsm = [jnp.where(ri >= ci, ws_ref[g], 0.0).astype(BF16) for g in range(n_groups)]
    for c in range(n_chunks):
        rows = slice(c * CHUNK, (c + 1) * CHUNK)
        for p in range(n_groups // 2):
            cols = slice(p * LANES, (p + 1) * LANES)
            vp = vb[rows, cols]
            s0 = jnp.dot(wsm[2 * p], vp, preferred_element_type=F32)
            s1 = jnp.dot(wsm[2 * p + 1], vp, preferred_element_type=F32)
            s = jnp.where(lane < HEAD, s0, s1) + bs_ref[:, cols]
            yc_ref[rows, cols] = u[rows, cols] * s


def _cmlp(pc, lng, lnb, ws, bs_rows, bd):
    t, p_c = pc.shape
    d_c = p_c // 2
    tm = _row_tile(t, 256)
    assert tm % CHUNK == 0
    return pl.pallas_call(
        functools.partial(_cmlp_kernel, d_c=d_c, n_chunks=tm // CHUNK),
        grid=(t // tm,),
        in_specs=[pl.BlockSpec((tm, p_c), lambda i: (i, 0)), _full((1, d_c)), _full((1, d_c)),
                  _full(ws.shape), _full(bs_rows.shape), _full(bd.shape)],
        out_specs=pl.BlockSpec((tm, d_c), lambda i: (i, 0)),
        out_shape=jax.ShapeDtypeStruct((t, d_c), F32),
        compiler_params=_cparams("parallel"),
        name="cmlp",
    )(pc, lng, lnb, ws, bs_rows, bd)


def _cmlp_first_kernel(pc_ref, lng_ref, lnb_ref, w00_ref, b0_ref, bd_ref, yc_ref, vn_ref, *, d_c):
    u, vn = _cmlp_norm(pc_ref[...], lng_ref[...], lnb_ref[...], bd_ref[...], d_c)
    yc_ref[...] = u * (w00_ref[...] * vn + b0_ref[...])
    vn_ref[...] = vn


def _cmlp_first(pc, lng, lnb, w00, b0, bd):
    t, p_c = pc.shape
    d_c = p_c // 2
    return pl.pallas_call(
        functools.partial(_cmlp_first_kernel, d_c=d_c),
        grid=(1,),
        in_specs=[_full(pc.shape), _full((1, d_c)), _full((1, d_c)), _full((1, d_c)), _full((1, d_c)),
                  _full(bd.shape)],
        out_specs=[_full((t, d_c))] * 2,
        out_shape=[jax.ShapeDtypeStruct((t, d_c), F32)] * 2,
        compiler_params=_cparams("arbitrary"),
        name="cmlp_first",
    )(pc, lng, lnb, w00, b0, bd)


def _mix_kernel(x_ref, y_ref, bonus_ref, g_ref, yc_ref, lng_ref, lnb_ref, bd_ref, wo_ref, o_ref, *, d_a):
    bd = bd_ref[...]
    y = y_ref[...]
    mean = _group_sum(y, bd) * (1.0 / HEAD)
    d = y - mean
    var = _group_sum(d * d, bd) * (1.0 / HEAD)
    yn = d * lax.rsqrt(var + GN_EPS) * lng_ref[...] + lnb_ref[...]
    ya = (yn + bonus_ref[...]) * g_ref[...]
    o_ref[...] = (x_ref[...]
                  + jnp.dot(ya.astype(BF16), wo_ref[0:d_a, :], preferred_element_type=F32)
                  + jnp.dot(yc_ref[...].astype(BF16), wo_ref[d_a:, :], preferred_element_type=F32))


def _mix(x, y, bonus, g, yc, lng, lnb, bd, wo_bf):
    t, d = x.shape
    d_a = y.shape[1]
    d_c = yc.shape[1]
    tm = _row_tile(t, 256)
    row = lambda n: pl.BlockSpec((tm, n), lambda i: (i, 0))
    return pl.pallas_call(
        functools.partial(_mix_kernel, d_a=d_a),
        grid=(t // tm,),
        in_specs=[row(d), row(d_a), row(d_a), row(d_a), row(d_c), _full((1, d_a)), _full((1, d_a)),
                  _full(bd.shape), _full(wo_bf.shape)],
        out_specs=row(d),
        out_shape=jax.ShapeDtypeStruct((t, d), F32),
        compiler_params=_cparams("parallel"),
        name="mix_out",
    )(x, y, bonus, g, yc, lng, lnb, bd, wo_bf)


def _peerq_kernel(x_ref, g_ref, wq_ref, sk_ref, xn_ref, st_ref, *, n_hc):
    hb = _rmsnorm(x_ref[...], g_ref[...]).astype(BF16)
    xn_ref[...] = hb
    qb = jnp.dot(hb, wq_ref[...], preferred_element_type=F32).astype(BF16)
    for hc in range(n_hc):
        st_ref[hc] = lax.dot_general(sk_ref[hc % 2], qb[:, hc * LANES:(hc + 1) * LANES],
                                     (((1,), (1,)), ((), ())), preferred_element_type=F32)


def _peerq(x, g, wq_bf, sk_bf):
    t, d = x.shape
    n_hc = wq_bf.shape[1] // LANES
    tm = _row_tile(t, 512)
    return pl.pallas_call(
        functools.partial(_peerq_kernel, n_hc=n_hc),
        grid=(t // tm,),
        in_specs=[pl.BlockSpec((tm, d), lambda i: (i, 0)), _full((1, d)), _full(wq_bf.shape),
                  _full(sk_bf.shape)],
        out_specs=[pl.BlockSpec((tm, d), lambda i: (i, 0)),
                   pl.BlockSpec((n_hc, N_KEYS, tm), lambda i: (0, 0, i))],
        out_shape=[jax.ShapeDtypeStruct((t, d), BF16), jax.ShapeDtypeStruct((n_hc, N_KEYS, t), F32)],
        compiler_params=_cparams("parallel"),
        name="peer_query",
    )(x, g, wq_bf, sk_bf)


def _batcher_network(n):
    pairs = []

    def merge(lo, m, r):
        step = 2 * r
        if step < m:
            merge(lo, m, step)
            merge(lo + r, m, step)
            pairs.extend((i, i + r) for i in range(lo + r, lo + m - r, step))
        else:
            pairs.append((lo, lo + r))

    def sort(lo, m):
        if m > 1:
            sort(lo, m // 2)
            sort(lo + m // 2, m // 2)
            merge(lo, m, 1)

    sort(0, n)
    return pairs


SUBLANES = 8
KEY_SLABS = N_KEYS // SUBLANES
SLAB_NETWORK = _batcher_network(KEY_SLABS)


def _pop_heads(lists, extra_heads, n):
    lists, extra_heads = list(lists), list(extra_heads)
    tops = []
    for r in range(n):
        head = lists[0]
        for e in extra_heads:
            head = jnp.maximum(head, e)
        m = jnp.max(head, axis=0, keepdims=True)
        tops.append(m)
        hit = lists[0] == m
        for d in range(min(n - 1 - r, len(lists))):
            nxt = lists[d + 1] if d + 1 < len(lists) else NEG_INF
            lists[d] = jnp.where(hit, nxt, lists[d])
        if r < n - 1:
            extra_heads = [jnp.where(e == m, NEG_INF, e) for e in extra_heads]
    return tops


def _top_sorted(s, n):
    slabs = [s[SUBLANES * r:SUBLANES * (r + 1)] for r in range(KEY_SLABS)]
    for i, j in SLAB_NETWORK:
        slabs[i], slabs[j] = jnp.maximum(slabs[i], slabs[j]), jnp.minimum(slabs[i], slabs[j])
    return _pop_heads(slabs, [], n)


def _peer_topk_kernel(st_ref, p1_ref, p2_ref, th_ref, *, n_heads):
    k = PEER_TOPK
    n = k + 1
    n_pad = -(-n // SUBLANES) * SUBLANES

    def head(h, carry):
        s1 = st_ref[2 * h]
        s2 = st_ref[2 * h + 1]
        t1 = _top_sorted(s1, n)
        t2 = _top_sorted(s2, n)
        row = lax.broadcasted_iota(jnp.int32, (n_pad, s1.shape[1]), 0)
        v1 = jnp.full((n_pad, s1.shape[1]), NEG_INF, F32)
        for i in range(n):
            v1 = jnp.where(row == i, t1[i], v1)
        row8 = row[0:SUBLANES]
        lists = [v1[0:SUBLANES] + t2[0]]
        for j in range(1, n):
            lists.append(jnp.where(row8 < n // (j + 1), v1[0:SUBLANES] + t2[j], NEG_INF))
        singles = [v1[SUBLANES * b:SUBLANES * (b + 1)] + t2[0] for b in range(1, n_pad // SUBLANES)]
        c = _pop_heads(lists, singles, n)
        m = c[0]
        z = jnp.ones_like(m)
        for r in range(1, k):
            z = z + jnp.exp(c[r] - m)
        inv_z = 1.0 / z
        th_ref[pl.ds(h, 1), :] = jnp.exp(0.5 * (c[k - 1] + c[k]) - m) * inv_z
        p1_ref[h] = jnp.exp(s1 - t1[0]) * inv_z
        p2_ref[h] = jnp.exp(s2 - t2[0])
        return carry

    lax.fori_loop(0, n_heads, head, 0)


def _peer_topk(st):
    n_hc, n, t = st.shape
    n_heads = n_hc // 2
    tk = LANES
    fac = pl.BlockSpec((n_heads, n, tk), lambda i: (0, 0, i))
    return pl.pallas_call(
        functools.partial(_peer_topk_kernel, n_heads=n_heads),
        grid=(t // tk,),
        in_specs=[pl.BlockSpec((n_hc, n, tk), lambda i: (0, 0, i))],
        out_specs=[fac, fac, pl.BlockSpec((n_heads, tk), lambda i: (0, i))],
        out_shape=[jax.ShapeDtypeStruct((n_heads, n, t), F32)] * 2 + [jax.ShapeDtypeStruct((n_heads, t), F32)],
        compiler_params=_cparams("parallel"),
        name="peer_topk",
    )(st)


ROW_BLK = 32
KEYS_PER_TILE = 8


def _peer_dense_kernel(xn_ref, x_ref, u_ref, v_ref, p1_ref, p2_ref, th_ref, o_ref, ht_ref, at_ref, acc_ref,
                       xnt_ref, *, n_heads, tm):
    j = pl.program_id(1)

    @pl.when(j == 0)
    def _():
        acc_ref[...] = jnp.zeros_like(acc_ref)
        xnt_ref[...] = xn_ref[...].T

    for k in range(KEYS_PER_TILE):
        rows_k = pl.ds(k * N_KEYS, N_KEYS)
        ht_ref[rows_k, :] = jnp.dot(u_ref[rows_k, :], xnt_ref[...], preferred_element_type=F32)
        for lc in range(tm // LANES):
            cols = pl.ds(lc * LANES, LANES)
            p1_rows = [p1_ref[h, k:k + 1, cols] for h in range(n_heads)]
            th_rows = [th_ref[h:h + 1, cols] for h in range(n_heads)]
            for rb in range(N_KEYS // ROW_BLK):
                r2 = pl.ds(rb * ROW_BLK, ROW_BLK)
                r0 = pl.ds(k * N_KEYS + rb * ROW_BLK, ROW_BLK)
                gsum = jnp.zeros((ROW_BLK, LANES), F32)
                for h in range(n_heads):
                    w = p1_rows[h] * p2_ref[h, r2, cols]
                    gsum = gsum + jnp.where(w >= th_rows[h], w, 0.0)
                at_ref[r0, cols] = (_gelu(ht_ref[r0, cols]) * gsum).astype(BF16)

    acc_ref[...] += lax.dot_general(at_ref[...], v_ref[...], (((0,), (0,)), ((), ())),
                                    preferred_element_type=F32)

    @pl.when(j == pl.num_programs(1) - 1)
    def _():
        o_ref[...] = x_ref[...] + acc_ref[...]


def _peer_dense(xn, x, u_bf, v_bf, p1, p2, th):
    t, d = x.shape
    n_exp = u_bf.shape[0]
    n_heads = p1.shape[0]
    tm = _row_tile(t, 512)
    te = KEYS_PER_TILE * N_KEYS
    assert n_exp == N_KEYS * N_KEYS
    return pl.pallas_call(
        functools.partial(_peer_dense_kernel, n_heads=n_heads, tm=tm),
        grid=(t // tm, n_exp // te),
        in_specs=[pl.BlockSpec((tm, d), lambda i, j: (i, 0)),
                  pl.BlockSpec((tm, d), lambda i, j: (i, 0)),
                  pl.BlockSpec((te, d), lambda i, j: (j, 0)),
                  pl.BlockSpec((te, d), lambda i, j: (j, 0)),
                  pl.BlockSpec((n_heads, KEYS_PER_TILE, tm), lambda i, j: (0, j, i)),
                  pl.BlockSpec((n_heads, N_KEYS, tm), lambda i, j: (0, 0, i)),
                  pl.BlockSpec((n_heads, tm), lambda i, j: (0, i))],
        out_specs=pl.BlockSpec((tm, d), lambda i, j: (i, 0)),
        out_shape=jax.ShapeDtypeStruct((t, d), F32),
        scratch_shapes=[pltpu.VMEM((te, tm), F32), pltpu.VMEM((te, tm), BF16), pltpu.VMEM((tm, d), F32),
                        pltpu.VMEM((d, tm), BF16)],
        compiler_params=_cparams("parallel", "arbitrary"),
        name="peer_dense",
    )(xn, x, u_bf, v_bf, p1, p2, th)


def _ple_kernel(x_ref, p_ref, g_ref, gw_ref, pw_ref, gf_ref, o_ref, *, final):
    x = x_ref[...]
    hn = _rmsnorm(x, g_ref[...])
    gate = jax.nn.sigmoid(jnp.dot(hn.astype(BF16), gw_ref[...], preferred_element_type=F32))
    e = jnp.dot(p_ref[...].astype(BF16), pw_ref[...], preferred_element_type=F32)
    x = x + e * gate
    o_ref[...] = _rmsnorm(x, gf_ref[...]) if final else x


def _ple(x, p, g, gw_bf, pw_bf, gf, final):
    t, d = x.shape
    dp = p.shape[1]
    tm = _row_tile(t, 512)
    return pl.pallas_call(
        functools.partial(_ple_kernel, final=final),
        grid=(t // tm,),
        in_specs=[pl.BlockSpec((tm, d), lambda i: (i, 0)), pl.BlockSpec((tm, dp), lambda i: (i, 0)),
                  _full((1, d)), _full(gw_bf.shape), _full(pw_bf.shape), _full((1, d))],
        out_specs=pl.BlockSpec((tm, d), lambda i: (i, 0)),
        out_shape=jax.ShapeDtypeStruct((t, d), F32),
        compiler_params=_cparams("parallel"),
        name="ple",
    )(x, p, g, gw_bf, pw_bf, gf)


def _rwkv_prompt_scan(parts, batch, seq, n_heads):
    r, k, v, w, a, b = (x.reshape(batch, seq, n_heads * HEAD) for x in parts)
    y, s_fin = _scan_tokens(w, a, b, k, r, v)
    s_fin = s_fin.reshape(HEAD // 2, HEAD, 2, n_heads, batch).transpose(4, 3, 2, 0, 1)
    return y.reshape(batch * seq, n_heads * HEAD), s_fin.reshape(batch, n_heads, HEAD, HEAD)


def _rwkv_sample_scan(parts, wkv_prev, batch, n_heads):
    bh = batch * n_heads
    assert bh % LANES == 0
    rows = lambda x: x.reshape(batch, n_heads, HEAD).transpose(2, 0, 1).reshape(1, HEAD, bh)
    r, k, v, w, a, b = parts
    s0 = wkv_prev.transpose(2, 3, 0, 1).reshape(HEAD, HEAD, bh)
    y, s_fin = _scan(rows(w), rows(a), rows(b), rows(k), rows(r), rows(v), s0)
    y = y.reshape(HEAD, batch, n_heads).transpose(1, 2, 0).reshape(batch, n_heads * HEAD)
    return y, s_fin.reshape(HEAD, HEAD, batch, n_heads).transpose(2, 3, 0, 1)


def _layer(x, p, wkv_prev, shift_prev, lw, gf, final):
    batch, seq, d = x.shape
    t = batch * seq
    d_a = lw["w0"].shape[1]
    n_heads = d_a // HEAD
    p_a = lw["mu"].shape[1]
    xt = x.reshape(t, d)

    pa, pc = _proj(xt, lw["norm_mix_g"], lw["w_in"], p_a)
    pa3 = pa.reshape(batch, seq, p_a)
    r, k, v, w, a, b, g, bonus = _prep(pa, shift_prev, seq, lw["mu"], lw["w0"], lw["a0"], lw["wa2"], lw["g2"],
                                       lw["k_k"], lw["k_a"], lw["r_k"], lw["bd"])
    if wkv_prev is None:
        y, wkv_new = _rwkv_prompt_scan((r, k, v, w, a, b), batch, seq, n_heads)
        yc = _cmlp(pc, lw["cmlp_ln_g"], lw["cmlp_ln_b"], lw["cmlp_ws"], lw["cmlp_bs_rows"], lw["bd"])
        v_rows = None
    else:
        assert seq == 1
        y, wkv_new = _rwkv_sample_scan((r, k, v, w, a, b), wkv_prev, batch, n_heads)
        yc, v_rows = _cmlp_first(pc, lw["cmlp_ln_g"], lw["cmlp_ln_b"], lw["cmlp_w00"], lw["cmlp_b0"], lw["bd"])
        v_rows = v_rows.reshape(batch, seq, -1)
    x1 = _mix(xt, y, bonus, g, yc, lw["rwkv_ln_g"], lw["rwkv_ln_b"], lw["bd"], lw["w_out"])

    xn, st = _peerq(x1, lw["norm_ffn_g"], lw["peer_wq"], lw["peer_subkeys"])
    p1, p2, th = _peer_topk(st)
    x2 = _peer_dense(xn, x1, lw["peer_u"], lw["peer_v"], p1, p2, th)

    x3 = _ple(x2, p.reshape(t, -1), lw["norm_ple_g"], lw["ple_gate_w"], lw["ple_w"], gf, final)
    return x3.reshape(batch, seq, d), wkv_new, pa3[:, -1], v_rows


def kernel(x_prompt, x_sample, state_wkv, state_shift, p_prompt, p_sample, norm_mix_g, w_in, shift_mu, rwkv_w0, rwkv_w2, rwkv_a0, rwkv_a2, rwkv_g2, rwkv_k_k, rwkv_k_a, rwkv_r_k, rwkv_ln_g, rwkv_ln_b, cmlp_ln_g, cmlp_ln_b, cmlp_ws, cmlp_bs, w_out, norm_ffn_g, peer_wq, peer_subkeys, peer_u, peer_v, norm_ple_g, ple_w, ple_gate_w, norm_final_g):
    depth = state_wkv.shape[0]
    batch = x_prompt.shape[0]
    d_a = rwkv_w0.shape[1]
    d_c = cmlp_ln_g.shape[1]
    p_a = shift_mu.shape[1]
    lora_w = rwkv_w2.shape[1]
    lora_a = rwkv_a2.shape[1]
    assert lora_w == HEAD and lora_a == HEAD and d_a == d_c
    gidx = jnp.arange(d_a) // HEAD
    bd = (gidx[:, None] == gidx[None, :]).astype(BF16)
    row = lambda z: z.reshape(1, -1)
    gf = row(norm_final_g)

    hp, hs = x_prompt, x_sample
    outs = [[] for _ in range(5)]
    for i in range(depth):
        wa2 = jnp.zeros((lora_w + lora_a, 2 * d_a), F32)
        wa2 = wa2.at[:lora_w, :d_a].set(rwkv_w2[i]).at[lora_w:, d_a:].set(rwkv_a2[i])
        lw = dict(
            norm_mix_g=row(norm_mix_g[i]), w_in=w_in[i].astype(BF16), mu=row(shift_mu[i]),
            w0=row(rwkv_w0[i]), a0=row(rwkv_a0[i]), wa2=wa2.astype(BF16), g2=rwkv_g2[i].astype(BF16),
            k_k=row(rwkv_k_k[i]), k_a=row(rwkv_k_a[i]), r_k=row(rwkv_r_k[i]), bd=bd,
            rwkv_ln_g=row(rwkv_ln_g[i]), rwkv_ln_b=row(rwkv_ln_b[i]),
            cmlp_ln_g=row(cmlp_ln_g[i]), cmlp_ln_b=row(cmlp_ln_b[i]), cmlp_ws=cmlp_ws[i],
            cmlp_bs_rows=jnp.repeat(cmlp_bs[i].T, HEAD, axis=1),
            cmlp_w00=row(jnp.repeat(cmlp_ws[i][:, 0, 0], HEAD)), cmlp_b0=row(jnp.repeat(cmlp_bs[i][:, 0], HEAD)),
            w_out=w_out[i].astype(BF16), norm_ffn_g=row(norm_ffn_g[i]), peer_wq=peer_wq[i].astype(BF16),
            peer_subkeys=peer_subkeys[i].astype(BF16), peer_u=peer_u[i].astype(BF16),
            peer_v=peer_v[i].astype(BF16), norm_ple_g=row(norm_ple_g[i]),
            ple_w=ple_w[i].astype(BF16), ple_gate_w=ple_gate_w[i].astype(BF16),
        )
        final = i == depth - 1
        shift0 = jnp.zeros((batch, p_a), F32)
        hp, wkv_p, shift_p, _ = _layer(hp, p_prompt[i], None, shift0, lw, gf, final)
        hs, wkv_s, shift_s, v_s = _layer(hs, p_sample[i], state_wkv[i], state_shift[i], lw, gf, final)
        for o, val in zip(outs, (wkv_p, shift_p, wkv_s, shift_s, v_s)):
            o.append(val)
    return (hp, hs) + tuple(jnp.stack(o) for o in outs)
```

```python
import functools
import math

import jax
import jax.numpy as jnp
from jax import lax
from jax.experimental import pallas as pl
from jax.experimental.pallas import tpu as pltpu

F32 = jnp.float32
BF16 = jnp.bfloat16

LANES = 128
HEAD = 64
CHUNK = 128
N_KEYS = 128
PEER_TOPK = 16
RMS_EPS = 1e-6
LN_EPS = 1e-5
GN_EPS = 64e-5
SQRT_HALF = math.sqrt(0.5)
NEG_INF = float("-inf")
VMEM_LIMIT = 56 * 1024 * 1024


def _cparams(*sem):
    return pltpu.CompilerParams(dimension_semantics=sem, vmem_limit_bytes=VMEM_LIMIT)


def _rmsnorm(x, g):
    return x * lax.rsqrt(jnp.mean(x * x, axis=-1, keepdims=True) + RMS_EPS) * g


def _gelu(x):
    return 0.5 * x * (1.0 + lax.erf(x * SQRT_HALF))


def _group_sum(x, bd):
    hi = x.astype(BF16)
    rest = x - hi.astype(F32)
    mid = rest.astype(BF16)
    lo = (rest - mid.astype(F32)).astype(BF16)
    return (jnp.dot(hi, bd, preferred_element_type=F32) + jnp.dot(mid, bd, preferred_element_type=F32)
            + jnp.dot(lo, bd, preferred_element_type=F32))


def _row_tile(n, want):
    t = min(n, want)
    assert n % t == 0, (n, t)
    return t


def _full(shape):
    return pl.BlockSpec(shape, lambda *_: (0,) * len(shape))


def _proj_kernel(x_ref, g_ref, w_ref, pa_ref, pc_ref, *, p_a):
    h = _rmsnorm(x_ref[...], g_ref[...])
    p = jnp.dot(h.astype(BF16), w_ref[...], preferred_element_type=F32)
    pa_ref[...] = p[:, :p_a]
    pc_ref[...] = p[:, p_a:]


def _proj(x, g, w_bf, p_a):
    t, d = x.shape
    p_tot = w_bf.shape[1]
    tm = _row_tile(t, 512)
    return pl.pallas_call(
        functools.partial(_proj_kernel, p_a=p_a),
        grid=(t // tm,),
        in_specs=[pl.BlockSpec((tm, d), lambda i: (i, 0)), _full((1, d)), _full((d, p_tot))],
        out_specs=[pl.BlockSpec((tm, p_a), lambda i: (i, 0)),
                   pl.BlockSpec((tm, p_tot - p_a), lambda i: (i, 0))],
        out_shape=[jax.ShapeDtypeStruct((t, p_a), F32), jax.ShapeDtypeStruct((t, p_tot - p_a), F32)],
        compiler_params=_cparams("parallel"),
        name="proj",
    )(x, g, w_bf)


def _prep_kernel(pa_ref, prev_ref, mu_ref, w0_ref, a0_ref, wa2_ref, g2_ref, kk_ref, ka_ref, rk_ref, bd_ref,
                 r_o, k_o, v_o, w_o, a_o, b_o, g_o, bonus_o, *carry, d_a):
    pa = pa_ref[...]
    if carry:
        (carry_ref,) = carry
        first = jnp.where(pl.program_id(1) == 0, prev_ref[0], carry_ref[...])
        row = lax.broadcasted_iota(jnp.int32, pa.shape, 0)
        prev = jnp.where(row == 0, first, pltpu.roll(pa, 1, 0))
        carry_ref[...] = pa[pa.shape[0] - 1:, :]
    else:
        prev = prev_ref[...]
    xm = pa + (prev - pa) * mu_ref[...]
    r = xm[:, 0:d_a]
    k = xm[:, d_a:2 * d_a]
    v = xm[:, 2 * d_a:3 * d_a]
    wal = xm[:, 3 * d_a:3 * d_a + LANES]
    gl = xm[:, 3 * d_a + LANES:]
    lane = lax.broadcasted_iota(jnp.int32, wal.shape, 1)
    wal = jnp.where(lane < HEAD, jnp.tanh(wal), wal)
    lo = jnp.dot(wal.astype(BF16), wa2_ref[...], preferred_element_type=F32)
    z = -(w0_ref[...] + lo[:, :d_a])
    softplus = jnp.maximum(z, 0.0) + jnp.log1p(jnp.exp(-jnp.abs(z)))
    decay = jnp.exp(-jnp.exp(-softplus - 0.5))
    a = jax.nn.sigmoid(a0_ref[...] + lo[:, d_a:])
    g = jnp.dot(jax.nn.sigmoid(gl).astype(BF16), g2_ref[...], preferred_element_type=F32)
    bd = bd_ref[...]
    kk = k * kk_ref[...]
    kk = kk / jnp.maximum(jnp.sqrt(_group_sum(kk * kk, bd)), 1e-12)
    kh = k * (1.0 + (a - 1.0) * ka_ref[...])
    r_o[...] = r
    k_o[...] = kh
    v_o[...] = v
    w_o[...] = decay
    a_o[...] = -kk
    b_o[...] = kk * a
    g_o[...] = g
    bonus_o[...] = _group_sum(r * kh * rk_ref[...], bd) * v


def _prep(pa, shift_prev, seq, mu, w0, a0, wa2_bf, g2_bf, k_k, k_a, r_k, bd):
    t, p_a = pa.shape
    batch = t // seq
    d_a = w0.shape[1]
    if seq == 1:
        tm = _row_tile(t, 256)
        grid = (t // tm, 1)
        prev, prev_spec, scratch = shift_prev, pl.BlockSpec((tm, p_a), lambda i, l: (i, 0)), []
    else:
        tm = _row_tile(seq, 256)
        grid = (batch, seq // tm)
        prev = shift_prev.reshape(batch, 1, p_a)
        prev_spec, scratch = pl.BlockSpec((1, 1, p_a), lambda b, l: (b, 0, 0)), [pltpu.VMEM((1, p_a), F32)]
    n_l = grid[1]
    row = lambda n: pl.BlockSpec((tm, n), lambda b, l: (b * n_l + l, 0))
    const = lambda shape: pl.BlockSpec(shape, lambda b, l: (0,) * len(shape))
    return pl.pallas_call(
        functools.partial(_prep_kernel, d_a=d_a),
        grid=grid,
        in_specs=[row(p_a), prev_spec, const((1, p_a)), const((1, d_a)), const((1, d_a)),
                  const(wa2_bf.shape), const(g2_bf.shape), const((1, d_a)), const((1, d_a)),
                  const((1, d_a)), const(bd.shape)],
        out_specs=[row(d_a)] * 8,
        out_shape=[jax.ShapeDtypeStruct((t, d_a), F32)] * 8,
        scratch_shapes=scratch,
        compiler_params=_cparams("parallel", "arbitrary"),
        name="rwkv_prep",
    )(pa, prev, mu, w0, a0, wa2_bf, g2_bf, k_k, k_a, r_k, bd)


def _recurrence(st_ref, w_ref, a_ref, b_ref, k_ref, r_ref, v_ref, y_ref, tl, nv):
    def step(t, carry):
        for i in range(nv):
            s = st_ref[i]
            sa = jnp.sum(s * a_ref[t], axis=0, keepdims=True)
            s = s * w_ref[t] + sa * b_ref[t] + v_ref[t, pl.ds(i, 1), :] * k_ref[t]
            st_ref[i] = s
            y_ref[t, pl.ds(i, 1), :] = jnp.sum(s * r_ref[t], axis=0, keepdims=True)
        return carry

    lax.fori_loop(0, tl, step, 0)


def _scan_kernel(w_ref, a_ref, b_ref, k_ref, r_ref, v_ref, s0_ref, y_ref, sfin_ref, st_ref, *, tl, nv):
    l = pl.program_id(1)

    @pl.when(l == 0)
    def _():
        st_ref[...] = s0_ref[...]

    _recurrence(st_ref, w_ref, a_ref, b_ref, k_ref, r_ref, v_ref, y_ref, tl, nv)

    @pl.when(l == pl.num_programs(1) - 1)
    def _():
        sfin_ref[...] = st_ref[...]


def _scan(w, a, b, k, r, v, s0):
    l, n, gl = w.shape
    nv = v.shape[1]
    tl = _row_tile(l, 64)
    rows = pl.BlockSpec((tl, n, LANES), lambda g, i: (i, 0, g))
    vals = pl.BlockSpec((tl, nv, LANES), lambda g, i: (i, 0, g))
    state = pl.BlockSpec((nv, n, LANES), lambda g, i: (0, 0, g))
    return pl.pallas_call(
        functools.partial(_scan_kernel, tl=tl, nv=nv),
        grid=(gl // LANES, l // tl),
        in_specs=[rows] * 5 + [vals, state],
        out_specs=[vals, state],
        out_shape=[jax.ShapeDtypeStruct((l, nv, gl), F32), jax.ShapeDtypeStruct((nv, n, gl), F32)],
        scratch_shapes=[pltpu.VMEM((nv, n, LANES), F32)],
        compiler_params=_cparams("parallel", "arbitrary"),
        name="rwkv_scan",
    )(w, a, b, k, r, v, s0)


def _keys_to_lanes(x, n_heads):
    pieces = [x[:, h * HEAD:(h + 1) * HEAD] for h in range(n_heads)]
    return jnp.concatenate(pieces + pieces, axis=0).T


def _values_to_lanes(x, n_heads):
    half = HEAD // 2
    lo = [x[:, h * HEAD:h * HEAD + half] for h in range(n_heads)]
    hi = [x[:, h * HEAD + half:(h + 1) * HEAD] for h in range(n_heads)]
    return jnp.concatenate(lo + hi, axis=0).T


def _values_from_lanes(y, n_heads, batch):
    yt = y.T
    hb = n_heads * batch
    full = jnp.concatenate([yt[0:hb], yt[hb:2 * hb]], axis=1)
    return jnp.concatenate([full[h * batch:(h + 1) * batch] for h in range(n_heads)], axis=1)


N_KEY_ARRAYS = 5
N_SLOTS = 8
LOOKAHEAD = 2


def _scan_tokens_kernel(*refs, tl, nv, n_heads, batch):
    key_refs, v_ref = refs[:N_KEY_ARRAYS], refs[N_KEY_ARRAYS]
    y_ref, sfin_ref, st_ref, sa_ref = refs[N_KEY_ARRAYS + 1:N_KEY_ARRAYS + 5]
    slot_refs = refs[N_KEY_ARRAYS + 5:]
    n_slot = N_KEY_ARRAYS + 2
    slots = [slot_refs[n_slot * q:n_slot * (q + 1)] for q in range(N_SLOTS)]
    l = pl.program_id(0)

    @pl.when(l == 0)
    def _():
        st_ref[...] = jnp.zeros_like(st_ref)
        slots[-1][-1][...] = jnp.zeros_like(slots[-1][-1])

    def fill(slot, t):
        for src, dst in zip(key_refs, slot[:N_KEY_ARRAYS]):
            dst[...] = _keys_to_lanes(src[:, t, :], n_heads)
        slot[N_KEY_ARRAYS][...] = _values_to_lanes(v_ref[:, t, :], n_heads)

    for q in range(LOOKAHEAD):
        fill(slots[q], q)

    def one_step(cur, ahead, behind, t_ahead, t_behind):
        wt, at, bt, kt, rt, vt, yt = cur

        def relayout(job):
            if job == 0:
                y_ref[:, t_behind, :] = _values_from_lanes(behind[-1][...], n_heads, batch)
            elif job <= N_KEY_ARRAYS:
                ahead[job - 1][...] = _keys_to_lanes(key_refs[job - 1][:, t_ahead, :], n_heads)
            elif job == N_KEY_ARRAYS + 1:
                ahead[N_KEY_ARRAYS][...] = _values_to_lanes(v_ref[:, t_ahead, :], n_heads)

        for i in range(nv):
            sa_ref[pl.ds(i, 1), :] = jnp.sum(st_ref[i] * at[...], axis=0, keepdims=True)
            if i < n_slot:
                relayout(i)
        for i in range(nv):
            s = st_ref[i] * wt[...] + sa_ref[pl.ds(i, 1), :] * bt[...] + vt[pl.ds(i, 1), :] * kt[...]
            st_ref[i] = s
            yt[pl.ds(i, 1), :] = jnp.sum(s * rt[...], axis=0, keepdims=True)

    def steps(j, carry):
        for q in range(N_SLOTS):
            t = N_SLOTS * j + q
            one_step(slots[q], slots[(q + LOOKAHEAD) % N_SLOTS], slots[(q - 1) % N_SLOTS],
                     jnp.minimum(t + LOOKAHEAD, tl - 1), jnp.maximum(t - 1, 0))
        return carry

    lax.fori_loop(0, tl // N_SLOTS, steps, 0)
    y_ref[:, tl - 1, :] = _values_from_lanes(slots[-1][-1][...], n_heads, batch)

    @pl.when(l == pl.num_programs(0) - 1)
    def _():
        sfin_ref[...] = st_ref[...]


def _scan_tokens(w, a, b, k, r, v):
    batch, l, d = w.shape
    n_heads = d // HEAD
    assert 2 * batch * n_heads == LANES
    nv = HEAD // 2
    tl = _row_tile(l, 64)
    tok = pl.BlockSpec((batch, tl, d), lambda i: (0, i, 0))
    state = pl.BlockSpec((nv, HEAD, LANES), lambda i: (0, 0, 0))
    assert tl % N_SLOTS == 0 and LOOKAHEAD < N_SLOTS - 1
    slot = [pltpu.VMEM((HEAD, LANES), F32)] * N_KEY_ARRAYS + [pltpu.VMEM((nv, LANES), F32)] * 2
    return pl.pallas_call(
        functools.partial(_scan_tokens_kernel, tl=tl, nv=nv, n_heads=n_heads, batch=batch),
        grid=(l // tl,),
        in_specs=[tok] * 6,
        out_specs=[tok, state],
        out_shape=[jax.ShapeDtypeStruct((batch, l, d), F32), jax.ShapeDtypeStruct((nv, HEAD, LANES), F32)],
        scratch_shapes=[pltpu.VMEM((nv, HEAD, LANES), F32), pltpu.VMEM((nv, LANES), F32)] + slot * N_SLOTS,
        compiler_params=_cparams("arbitrary"),
        name="rwkv_scan_tokens",
    )(w, a, b, k, r, v)


def _cmlp_norm(pc, lng, lnb, bd, d_c):
    ge = _gelu(pc)
    u = ge[:, :d_c]
    v = ge[:, d_c:]
    mean = _group_sum(v, bd) * (1.0 / HEAD)
    d = v - mean
    var = _group_sum(d * d, bd) * (1.0 / HEAD)
    return u, d * lax.rsqrt(var + LN_EPS) * lng + lnb


def _cmlp_kernel(pc_ref, lng_ref, lnb_ref, ws_ref, bs_ref, bd_ref, yc_ref, *, d_c, n_chunks):
    u, vn = _cmlp_norm(pc_ref[...], lng_ref[...], lnb_ref[...], bd_ref[...], d_c)
    vb = vn.astype(BF16)
    ri = lax.broadcasted_iota(jnp.int32, (CHUNK, CHUNK), 0)
    ci = lax.broadcasted_iota(jnp.int32, (CHUNK, CHUNK), 1)
    lane = lax.broadcasted_iota(jnp.int32, (CHUNK, LANES), 1)
    n_groups = d_c // HEAD
    wsm = [jnp.where(ri >= ci, ws_ref[g], 0.0).astype(BF16) for g in range(n_groups)]
    for c in range(n_chunks):
        rows = slice(c * CHUNK, (c + 1) * CHUNK)
        for p in range(n_groups // 2):
            cols = slice(p * LANES, (p + 1) * LANES)
            vp = vb[rows, cols]
            s0 = jnp.dot(wsm[2 * p], vp, preferred_element_type=F32)
            s1 = jnp.dot(wsm[2 * p + 1], vp, preferred_element_type=F32)
            s = jnp.where(lane < HEAD, s0, s1) + bs_ref[:, cols]
            yc_ref[rows, cols] = u[rows, cols] * s


def _cmlp(pc, lng, lnb, ws, bs_rows, bd):
    t, p_c = pc.shape
    d_c = p_c // 2
    tm = _row_tile(t, 256)
    assert tm % CHUNK == 0
    return pl.pallas_call(
        functools.partial(_cmlp_kernel, d_c=d_c, n_chunks=tm // CHUNK),
        grid=(t // tm,),
        in_specs=[pl.BlockSpec((tm, p_c), lambda i: (i, 0)), _full((1, d_c)), _full((1, d_c)),
                  _full(ws.shape), _full(bs_rows.shape), _full(bd.shape)],
        out_specs=pl.BlockSpec((tm, d_c), lambda i: (i, 0)),
        out_shape=jax.ShapeDtypeStruct((t, d_c), F32),
        compiler_params=_cparams("parallel"),
        name="cmlp",
    )(pc, lng, lnb, ws, bs_rows, bd)


def _cmlp_first_kernel(pc_ref, lng_ref, lnb_ref, w00_ref, b0_ref, bd_ref, yc_ref, vn_ref, *, d_c):
    u, vn = _cmlp_norm(pc_ref[...], lng_ref[...], lnb_ref[...], bd_ref[...], d_c)
    yc_ref[...] = u * (w00_ref[...] * vn + b0_ref[...])
    vn_ref[...] = vn


def _cmlp_first(pc, lng, lnb, w00, b0, bd):
    t, p_c = pc.shape
    d_c = p_c // 2
    return pl.pallas_call(
        functools.partial(_cmlp_first_kernel, d_c=d_c),
        grid=(1,),
        in_specs=[_full(pc.shape), _full((1, d_c)), _full((1, d_c)), _full((1, d_c)), _full((1, d_c)),
                  _full(bd.shape)],
        out_specs=[_full((t, d_c))] * 2,
        out_shape=[jax.ShapeDtypeStruct((t, d_c), F32)] * 2,
        compiler_params=_cparams("arbitrary"),
        name="cmlp_first",
    )(pc, lng, lnb, w00, b0, bd)


def _mix_kernel(x_ref, y_ref, bonus_ref, g_ref, yc_ref, lng_ref, lnb_ref, bd_ref, wo_ref, o_ref, *, d_a):
    bd = bd_ref[...]
    y = y_ref[...]
    mean = _group_sum(y, bd) * (1.0 / HEAD)
    d = y - mean
    var = _group_sum(d * d, bd) * (1.0 / HEAD)
    yn = d * lax.rsqrt(var + GN_EPS) * lng_ref[...] + lnb_ref[...]
    ya = (yn + bonus_ref[...]) * g_ref[...]
    o_ref[...] = (x_ref[...]
                  + jnp.dot(ya.astype(BF16), wo_ref[0:d_a, :], preferred_element_type=F32)
                  + jnp.dot(yc_ref[...].astype(BF16), wo_ref[d_a:, :], preferred_element_type=F32))


def _mix(x, y, bonus, g, yc, lng, lnb, bd, wo_bf):
    t, d = x.shape
    d_a = y.shape[1]
    d_c = yc.shape[1]
    tm = _row_tile(t, 256)
    row = lambda n: pl.BlockSpec((tm, n), lambda i: (i, 0))
    return pl.pallas_call(
        functools.partial(_mix_kernel, d_a=d_a),
        grid=(t // tm,),
        in_specs=[row(d), row(d_a), row(d_a), row(d_a), row(d_c), _full((1, d_a)), _full((1, d_a)),
                  _full(bd.shape), _full(wo_bf.shape)],
        out_specs=row(d),
        out_shape=jax.ShapeDtypeStruct((t, d), F32),
        compiler_params=_cparams("parallel"),
        name="mix_out",
    )(x, y, bonus, g, yc, lng, lnb, bd, wo_bf)


def _peerq_kernel(x_ref, g_ref, wq_ref, sk_ref, xn_ref, st_ref, *, n_hc):
    hb = _rmsnorm(x_ref[...], g_ref[...]).astype(BF16)
    xn_ref[...] = hb
    qb = jnp.dot(hb, wq_ref[...], preferred_element_type=F32).astype(BF16)
    for hc in range(n_hc):
        st_ref[hc] = lax.dot_general(sk_ref[hc % 2], qb[:, hc * LANES:(hc + 1) * LANES],
                                     (((1,), (1,)), ((), ())), preferred_element_type=F32)


def _peerq(x, g, wq_bf, sk_bf):
    t, d = x.shape
    n_hc = wq_bf.shape[1] // LANES
    tm = _row_tile(t, 512)
    return pl.pallas_call(
        functools.partial(_peerq_kernel, n_hc=n_hc),
        grid=(t // tm,),
        in_specs=[pl.BlockSpec((tm, d), lambda i: (i, 0)), _full((1, d)), _full(wq_bf.shape),
                  _full(sk_bf.shape)],
        out_specs=[pl.BlockSpec((tm, d), lambda i: (i, 0)),
                   pl.BlockSpec((n_hc, N_KEYS, tm), lambda i: (0, 0, i))],
        out_shape=[jax.ShapeDtypeStruct((t, d), BF16), jax.ShapeDtypeStruct((n_hc, N_KEYS, t), F32)],
        compiler_params=_cparams("parallel"),
        name="peer_query",
    )(x, g, wq_bf, sk_bf)


def _batcher_network(n):
    pairs = []

    def merge(lo, m, r):
        step = 2 * r
        if step < m:
            merge(lo, m, step)
            merge(lo + r, m, step)
            pairs.extend((i, i + r) for i in range(lo + r, lo + m - r, step))
        else:
            pairs.append((lo, lo + r))

    def sort(lo, m):
        if m > 1:
            sort(lo, m // 2)
            sort(lo + m // 2, m // 2)
            merge(lo, m, 1)

    sort(0, n)
    return pairs


SUBLANES = 8
KEY_SLABS = N_KEYS // SUBLANES
SLAB_NETWORK = _batcher_network(KEY_SLABS)


def _pop_heads(lists, extra_heads, n):
    lists, extra_heads = list(lists), list(extra_heads)
    tops = []
    for r in range(n):
        head = lists[0]
        for e in extra_heads:
            head = jnp.maximum(head, e)
        m = jnp.max(head, axis=0, keepdims=True)
        tops.append(m)
        hit = lists[0] == m
        for d in range(min(n - 1 - r, len(lists))):
            nxt = lists[d + 1] if d + 1 < len(lists) else NEG_INF
            lists[d] = jnp.where(hit, nxt, lists[d])
        if r < n - 1:
            extra_heads = [jnp.where(e == m, NEG_INF, e) for e in extra_heads]
    return tops


def _top_sorted(s, n):
    slabs = [s[SUBLANES * r:SUBLANES * (r + 1)] for r in range(KEY_SLABS)]
    for i, j in SLAB_NETWORK:
        slabs[i], slabs[j] = jnp.maximum(slabs[i], slabs[j]), jnp.minimum(slabs[i], slabs[j])
    return _pop_heads(slabs, [], n)


def _peer_topk_kernel(st_ref, p1_ref, p2_ref, th_ref, *, n_heads):
    k = PEER_TOPK
    n = k + 1
    n_pad = -(-n // SUBLANES) * SUBLANES

    def head(h, carry):
        s1 = st_ref[2 * h]
        s2 = st_ref[2 * h + 1]
        t1 = _top_sorted(s1, n)
        t2 = _top_sorted(s2, n)
        row = lax.broadcasted_iota(jnp.int32, (n_pad, s1.shape[1]), 0)
        v1 = jnp.full((n_pad, s1.shape[1]), NEG_INF, F32)
        for i in range(n):
            v1 = jnp.where(row == i, t1[i], v1)
        row8 = row[0:SUBLANES]
        lists = [v1[0:SUBLANES] + t2[0]]
        for j in range(1, n):
            lists.append(jnp.where(row8 < n // (j + 1), v1[0:SUBLANES] + t2[j], NEG_INF))
        singles = [v1[SUBLANES * b:SUBLANES * (b + 1)] + t2[0] for b in range(1, n_pad // SUBLANES)]
        c = _pop_heads(lists, singles, n)
        m = c[0]
        z = jnp.ones_like(m)
        for r in range(1, k):
            z = z + jnp.exp(c[r] - m)
        inv_z = 1.0 / z
        th_ref[pl.ds(h, 1), :] = jnp.exp(0.5 * (c[k - 1] + c[k]) - m) * inv_z
        p1_ref[h] = jnp.exp(s1 - t1[0]) * inv_z
        p2_ref[h] = jnp.exp(s2 - t2[0])
        return carry

    lax.fori_loop(0, n_heads, head, 0)


def _peer_topk(st):
    n_hc, n, t = st.shape
    n_heads = n_hc // 2
    tk = LANES
    fac = pl.BlockSpec((n_heads, n, tk), lambda i: (0, 0, i))
    return pl.pallas_call(
        functools.partial(_peer_topk_kernel, n_heads=n_heads),
        grid=(t // tk,),
        in_specs=[pl.BlockSpec((n_hc, n, tk), lambda i: (0, 0, i))],
        out_specs=[fac, fac, pl.BlockSpec((n_heads, tk), lambda i: (0, i))],
        out_shape=[jax.ShapeDtypeStruct((n_heads, n, t), F32)] * 2 + [jax.ShapeDtypeStruct((n_heads, t), F32)],
        compiler_params=_cparams("parallel"),
        name="peer_topk",
    )(st)


ROW_BLK = 32
KEYS_PER_TILE = 8


def _peer_dense_kernel(xn_ref, x_ref, u_ref, v_ref, p1_ref, p2_ref, th_ref, o_ref, ht_ref, at_ref, acc_ref,
                       xnt_ref, *, n_heads, tm):
    j = pl.program_id(1)

    @pl.when(j == 0)
    def _():
        acc_ref[...] = jnp.zeros_like(acc_ref)
        xnt_ref[...] = xn_ref[...].T

    ht_ref[...] = jnp.dot(u_ref[...], xnt_ref[...], preferred_element_type=F32)

    for k in range(KEYS_PER_TILE):
        for lc in range(tm // LANES):
            cols = pl.ds(lc * LANES, LANES)
            p1_rows = [p1_ref[h, k:k + 1, cols] for h in range(n_heads)]
            th_rows = [th_ref[h:h + 1, cols] for h in range(n_heads)]
            for rb in range(N_KEYS // ROW_BLK):
                r2 = pl.ds(rb * ROW_BLK, ROW_BLK)
                r0 = pl.ds(k * N_KEYS + rb * ROW_BLK, ROW_BLK)
                gsum = jnp.zeros((ROW_BLK, LANES), F32)
                for h in range(n_heads):
                    w = p1_rows[h] * p2_ref[h, r2, cols]
                    gsum = gsum + jnp.where(w >= th_rows[h], w, 0.0)
                at_ref[r0, cols] = (_gelu(ht_ref[r0, cols]) * gsum).astype(BF16)

    acc_ref[...] += lax.dot_general(at_ref[...], v_ref[...], (((0,), (0,)), ((), ())),
                                    preferred_element_type=F32)

    @pl.when(j == pl.num_programs(1) - 1)
    def _():
        o_ref[...] = x_ref[...] + acc_ref[...]


def _peer_dense(xn, x, u_bf, v_bf, p1, p2, th):
    t, d = x.shape
    n_exp = u_bf.shape[0]
    n_heads = p1.shape[0]
    tm = _row_tile(t, 512)
    te = KEYS_PER_TILE * N_KEYS
    assert n_exp == N_KEYS * N_KEYS
    return pl.pallas_call(
        functools.partial(_peer_dense_kernel, n_heads=n_heads, tm=tm),
        grid=(t // tm, n_exp // te),
        in_specs=[pl.BlockSpec((tm, d), lambda i, j: (i, 0)),
                  pl.BlockSpec((tm, d), lambda i, j: (i, 0)),
                  pl.BlockSpec((te, d), lambda i, j: (j, 0)),
                  pl.BlockSpec((te, d), lambda i, j: (j, 0)),
                  pl.BlockSpec((n_heads, KEYS_PER_TILE, tm), lambda i, j: (0, j, i)),
                  pl.BlockSpec((n_heads, N_KEYS, tm), lambda i, j: (0, 0, i)),
                  pl.BlockSpec((n_heads, tm), lambda i, j: (0, i))],
        out_specs=pl.BlockSpec((tm, d), lambda i, j: (i, 0)),
        out_shape=jax.ShapeDtypeStruct((t, d), F32),
        scratch_shapes=[pltpu.VMEM((te, tm), F32), pltpu.VMEM((te, tm), BF16), pltpu.VMEM((tm, d), F32),
                        pltpu.VMEM((d, tm), BF16)],
        compiler_params=_cparams("parallel", "arbitrary"),
        name="peer_dense",
    )(xn, x, u_bf, v_bf, p1, p2, th)


def _ple_kernel(x_ref, p_ref, g_ref, gw_ref, pw_ref, gf_ref, o_ref, *, final):
    x = x_ref[...]
    hn = _rmsnorm(x, g_ref[...])
    gate = jax.nn.sigmoid(jnp.dot(hn.astype(BF16), gw_ref[...], preferred_element_type=F32))
    e = jnp.dot(p_ref[...].astype(BF16), pw_ref[...], preferred_element_type=F32)
    x = x + e * gate
    o_ref[...] = _rmsnorm(x, gf_ref[...]) if final else x


def _ple(x, p, g, gw_bf, pw_bf, gf, final):
    t, d = x.shape
    dp = p.shape[1]
    tm = _row_tile(t, 512)
    return pl.pallas_call(
        functools.partial(_ple_kernel, final=final),
        grid=(t // tm,),
        in_specs=[pl.BlockSpec((tm, d), lambda i: (i, 0)), pl.BlockSpec((tm, dp), lambda i: (i, 0)),
                  _full((1, d)), _full(gw_bf.shape), _full(pw_bf.shape), _full((1, d))],
        out_specs=pl.BlockSpec((tm, d), lambda i: (i, 0)),
        out_shape=jax.ShapeDtypeStruct((t, d), F32),
        compiler_params=_cparams("parallel"),
        name="ple",
    )(x, p, g, gw_bf, pw_bf, gf)


def _rwkv_prompt_scan(parts, batch, seq, n_heads):
    r, k, v, w, a, b = (x.reshape(batch, seq, n_heads * HEAD) for x in parts)
    y, s_fin = _scan_tokens(w, a, b, k, r, v)
    s_fin = s_fin.reshape(HEAD // 2, HEAD, 2, n_heads, batch).transpose(4, 3, 2, 0, 1)
    return y.reshape(batch * seq, n_heads * HEAD), s_fin.reshape(batch, n_heads, HEAD, HEAD)


def _rwkv_sample_scan(parts, wkv_prev, batch, n_heads):
    bh = batch * n_heads
    assert bh % LANES == 0
    rows = lambda x: x.reshape(batch, n_heads, HEAD).transpose(2, 0, 1).reshape(1, HEAD, bh)
    r, k, v, w, a, b = parts
    s0 = wkv_prev.transpose(2, 3, 0, 1).reshape(HEAD, HEAD, bh)
    y, s_fin = _scan(rows(w), rows(a), rows(b), rows(k), rows(r), rows(v), s0)
    y = y.reshape(HEAD, batch, n_heads).transpose(1, 2, 0).reshape(batch, n_heads * HEAD)
    return y, s_fin.reshape(HEAD, HEAD, batch, n_heads).transpose(2, 3, 0, 1)


def _layer(x, p, wkv_prev, shift_prev, lw, gf, final):
    batch, seq, d = x.shape
    t = batch * seq
    d_a = lw["w0"].shape[1]
    n_heads = d_a // HEAD
    p_a = lw["mu"].shape[1]
    xt = x.reshape(t, d)

    pa, pc = _proj(xt, lw["norm_mix_g"], lw["w_in"], p_a)
    pa3 = pa.reshape(batch, seq, p_a)
    r, k, v, w, a, b, g, bonus = _prep(pa, shift_prev, seq, lw["mu"], lw["w0"], lw["a0"], lw["wa2"], lw["g2"],
                                       lw["k_k"], lw["k_a"], lw["r_k"], lw["bd"])
    if wkv_prev is None:
        y, wkv_new = _rwkv_prompt_scan((r, k, v, w, a, b), batch, seq, n_heads)
        yc = _cmlp(pc, lw["cmlp_ln_g"], lw["cmlp_ln_b"], lw["cmlp_ws"], lw["cmlp_bs_rows"], lw["bd"])
        v_rows = None
    else:
        assert seq == 1
        y, wkv_new = _rwkv_sample_scan((r, k, v, w, a, b), wkv_prev, batch, n_heads)
        yc, v_rows = _cmlp_first(pc, lw["cmlp_ln_g"], lw["cmlp_ln_b"], lw["cmlp_w00"], lw["cmlp_b0"], lw["bd"])
        v_rows = v_rows.reshape(batch, seq, -1)
    x1 = _mix(xt, y, bonus, g, yc, lw["rwkv_ln_g"], lw["rwkv_ln_b"], lw["bd"], lw["w_out"])

    xn, st = _peerq(x1, lw["norm_ffn_g"], lw["peer_wq"], lw["peer_subkeys"])
    p1, p2, th = _peer_topk(st)
    x2 = _peer_dense(xn, x1, lw["peer_u"], lw["peer_v"], p1, p2, th)

    x3 = _ple(x2, p.reshape(t, -1), lw["norm_ple_g"], lw["ple_gate_w"], lw["ple_w"], gf, final)
    return x3.reshape(batch, seq, d), wkv_new, pa3[:, -1], v_rows


def kernel(x_prompt, x_sample, state_wkv, state_shift, p_prompt, p_sample, norm_mix_g, w_in, shift_mu, rwkv_w0, rwkv_w2, rwkv_a0, rwkv_a2, rwkv_g2, rwkv_k_k, rwkv_k_a, rwkv_r_k, rwkv_ln_g, rwkv_ln_b, cmlp_ln_g, cmlp_ln_b, cmlp_ws, cmlp_bs, w_out, norm_ffn_g, peer_wq, peer_subkeys, peer_u, peer_v, norm_ple_g, ple_w, ple_gate_w, norm_final_g):
    depth = state_wkv.shape[0]
    batch = x_prompt.shape[0]
    d_a = rwkv_w0.shape[1]
    d_c = cmlp_ln_g.shape[1]
    p_a = shift_mu.shape[1]
    lora_w = rwkv_w2.shape[1]
    lora_a = rwkv_a2.shape[1]
    assert lora_w == HEAD and lora_a == HEAD and d_a == d_c
    gidx = jnp.arange(d_a) // HEAD
    bd = (gidx[:, None] == gidx[None, :]).astype(BF16)
    row = lambda z: z.reshape(1, -1)
    gf = row(norm_final_g)

    hp, hs = x_prompt, x_sample
    outs = [[] for _ in range(5)]
    for i in range(depth):
        wa2 = jnp.zeros((lora_w + lora_a, 2 * d_a), F32)
        wa2 = wa2.at[:lora_w, :d_a].set(rwkv_w2[i]).at[lora_w:, d_a:].set(rwkv_a2[i])
        lw = dict(
            norm_mix_g=row(norm_mix_g[i]), w_in=w_in[i].astype(BF16), mu=row(shift_mu[i]),
            w0=row(rwkv_w0[i]), a0=row(rwkv_a0[i]), wa2=wa2.astype(BF16), g2=rwkv_g2[i].astype(BF16),
            k_k=row(rwkv_k_k[i]), k_a=row(rwkv_k_a[i]), r_k=row(rwkv_r_k[i]), bd=bd,
            rwkv_ln_g=row(rwkv_ln_g[i]), rwkv_ln_b=row(rwkv_ln_b[i]),
            cmlp_ln_g=row(cmlp_ln_g[i]), cmlp_ln_b=row(cmlp_ln_b[i]), cmlp_ws=cmlp_ws[i],
            cmlp_bs_rows=jnp.repeat(cmlp_bs[i].T, HEAD, axis=1),
            cmlp_w00=row(jnp.repeat(cmlp_ws[i][:, 0, 0], HEAD)), cmlp_b0=row(jnp.repeat(cmlp_bs[i][:, 0], HEAD)),
            w_out=w_out[i].astype(BF16), norm_ffn_g=row(norm_ffn_g[i]), peer_wq=peer_wq[i].astype(BF16),
            peer_subkeys=peer_subkeys[i].astype(BF16), peer_u=peer_u[i].astype(BF16),
            peer_v=peer_v[i].astype(BF16), norm_ple_g=row(norm_ple_g[i]),
            ple_w=ple_w[i].astype(BF16), ple_gate_w=ple_gate_w[i].astype(BF16),
        )
        final = i == depth - 1
        shift0 = jnp.zeros((batch, p_a), F32)
        hp, wkv_p, shift_p, _ = _layer(hp, p_prompt[i], None, shift0, lw, gf, final)
        hs, wkv_s, shift_s, v_s = _layer(hs, p_sample[i], state_wkv[i], state_shift[i], lw, gf, final)
        for o, val in zip(outs, (wkv_p, shift_p, wkv_s, shift_s, v_s)):
            o.append(val)
    return (hp, hs) + tuple(jnp.stack(o) for o in outs)
```

```python
import functools
import math

import jax
import jax.numpy as jnp
from jax import lax
from jax.experimental import pallas as pl
from jax.experimental.pallas import tpu as pltpu

F32 = jnp.float32
BF16 = jnp.bfloat16

LANES = 128
HEAD = 64
CHUNK = 128
N_KEYS = 128
PEER_TOPK = 16
RMS_EPS = 1e-6
LN_EPS = 1e-5
GN_EPS = 64e-5
SQRT_HALF = math.sqrt(0.5)
NEG_INF = float("-inf")
VMEM_LIMIT = 56 * 1024 * 1024


def _cparams(*sem):
    return pltpu.CompilerParams(dimension_semantics=sem, vmem_limit_bytes=VMEM_LIMIT)


def _rmsnorm(x, g):
    return x * lax.rsqrt(jnp.mean(x * x, axis=-1, keepdims=True) + RMS_EPS) * g


def _gelu(x):
    return 0.5 * x * (1.0 + lax.erf(x * SQRT_HALF))


def _group_sum(x, bd):
    hi = x.astype(BF16)
    rest = x - hi.astype(F32)
    mid = rest.astype(BF16)
    lo = (rest - mid.astype(F32)).astype(BF16)
    return (jnp.dot(hi, bd, preferred_element_type=F32) + jnp.dot(mid, bd, preferred_element_type=F32)
            + jnp.dot(lo, bd, preferred_element_type=F32))


def _row_tile(n, want):
    t = min(n, want)
    assert n % t == 0, (n, t)
    return t


def _full(shape):
    return pl.BlockSpec(shape, lambda *_: (0,) * len(shape))


def _proj_kernel(x_ref, g_ref, w_ref, pa_ref, pc_ref, *, p_a):
    h = _rmsnorm(x_ref[...], g_ref[...])
    p = jnp.dot(h.astype(BF16), w_ref[...], preferred_element_type=F32)
    pa_ref[...] = p[:, :p_a]
    pc_ref[...] = p[:, p_a:]


def _proj(x, g, w_bf, p_a):
    t, d = x.shape
    p_tot = w_bf.shape[1]
    tm = _row_tile(t, 512)
    return pl.pallas_call(
        functools.partial(_proj_kernel, p_a=p_a),
        grid=(t // tm,),
        in_specs=[pl.BlockSpec((tm, d), lambda i: (i, 0)), _full((1, d)), _full((d, p_tot))],
        out_specs=[pl.BlockSpec((tm, p_a), lambda i: (i, 0)),
                   pl.BlockSpec((tm, p_tot - p_a), lambda i: (i, 0))],
        out_shape=[jax.ShapeDtypeStruct((t, p_a), F32), jax.ShapeDtypeStruct((t, p_tot - p_a), F32)],
        compiler_params=_cparams("parallel"),
        name="proj",
    )(x, g, w_bf)


def _prep_kernel(pa_ref, prev_ref, mu_ref, w0_ref, a0_ref, wa2_ref, g2_ref, kk_ref, ka_ref, rk_ref, bd_ref,
                 r_o, k_o, v_o, w_o, a_o, b_o, g_o, bonus_o, *carry, d_a):
    pa = pa_ref[...]
    if carry:
        (carry_ref,) = carry
        first = jnp.where(pl.program_id(1) == 0, prev_ref[0], carry_ref[...])
        row = lax.broadcasted_iota(jnp.int32, pa.shape, 0)
        prev = jnp.where(row == 0, first, pltpu.roll(pa, 1, 0))
        carry_ref[...] = pa[pa.shape[0] - 1:, :]
    else:
        prev = prev_ref[...]
    xm = pa + (prev - pa) * mu_ref[...]
    r = xm[:, 0:d_a]
    k = xm[:, d_a:2 * d_a]
    v = xm[:, 2 * d_a:3 * d_a]
    wal = xm[:, 3 * d_a:3 * d_a + LANES]
    gl = xm[:, 3 * d_a + LANES:]
    lane = lax.broadcasted_iota(jnp.int32, wal.shape, 1)
    wal = jnp.where(lane < HEAD, jnp.tanh(wal), wal)
    lo = jnp.dot(wal.astype(BF16), wa2_ref[...], preferred_element_type=F32)
    z = -(w0_ref[...] + lo[:, :d_a])
    softplus = jnp.maximum(z, 0.0) + jnp.log1p(jnp.exp(-jnp.abs(z)))
    decay = jnp.exp(-jnp.exp(-softplus - 0.5))
    a = jax.nn.sigmoid(a0_ref[...] + lo[:, d_a:])
    g = jnp.dot(jax.nn.sigmoid(gl).astype(BF16), g2_ref[...], preferred_element_type=F32)
    bd = bd_ref[...]
    kk = k * kk_ref[...]
    kk = kk / jnp.maximum(jnp.sqrt(_group_sum(kk * kk, bd)), 1e-12)
    kh = k * (1.0 + (a - 1.0) * ka_ref[...])
    r_o[...] = r
    k_o[...] = kh
    v_o[...] = v
    w_o[...] = decay
    a_o[...] = -kk
    b_o[...] = kk * a
    g_o[...] = g
    bonus_o[...] = _group_sum(r * kh * rk_ref[...], bd) * v


def _prep(pa, shift_prev, seq, mu, w0, a0, wa2_bf, g2_bf, k_k, k_a, r_k, bd):
    t, p_a = pa.shape
    batch = t // seq
    d_a = w0.shape[1]
    if seq == 1:
        tm = _row_tile(t, 256)
        grid = (t // tm, 1)
        prev, prev_spec, scratch = shift_prev, pl.BlockSpec((tm, p_a), lambda i, l: (i, 0)), []
    else:
        tm = _row_tile(seq, 256)
        grid = (batch, seq // tm)
        prev = shift_prev.reshape(batch, 1, p_a)
        prev_spec, scratch = pl.BlockSpec((1, 1, p_a), lambda b, l: (b, 0, 0)), [pltpu.VMEM((1, p_a), F32)]
    n_l = grid[1]
    row = lambda n: pl.BlockSpec((tm, n), lambda b, l: (b * n_l + l, 0))
    const = lambda shape: pl.BlockSpec(shape, lambda b, l: (0,) * len(shape))
    return pl.pallas_call(
        functools.partial(_prep_kernel, d_a=d_a),
        grid=grid,
        in_specs=[row(p_a), prev_spec, const((1, p_a)), const((1, d_a)), const((1, d_a)),
                  const(wa2_bf.shape), const(g2_bf.shape), const((1, d_a)), const((1, d_a)),
                  const((1, d_a)), const(bd.shape)],
        out_specs=[row(d_a)] * 8,
        out_shape=[jax.ShapeDtypeStruct((t, d_a), F32)] * 8,
        scratch_shapes=scratch,
        compiler_params=_cparams("parallel", "arbitrary"),
        name="rwkv_prep",
    )(pa, prev, mu, w0, a0, wa2_bf, g2_bf, k_k, k_a, r_k, bd)


def _recurrence(st_ref, w_ref, a_ref, b_ref, k_ref, r_ref, v_ref, y_ref, tl, nv):
    def step(t, carry):
        for i in range(nv):
            s = st_ref[i]
            sa = jnp.sum(s * a_ref[t], axis=0, keepdims=True)
            s = s * w_ref[t] + sa * b_ref[t] + v_ref[t, pl.ds(i, 1), :] * k_ref[t]
            st_ref[i] = s
            y_ref[t, pl.ds(i, 1), :] = jnp.sum(s * r_ref[t], axis=0, keepdims=True)
        return carry

    lax.fori_loop(0, tl, step, 0)


def _scan_kernel(w_ref, a_ref, b_ref, k_ref, r_ref, v_ref, s0_ref, y_ref, sfin_ref, st_ref, *, tl, nv):
    l = pl.program_id(1)

    @pl.when(l == 0)
    def _():
        st_ref[...] = s0_ref[...]

    _recurrence(st_ref, w_ref, a_ref, b_ref, k_ref, r_ref, v_ref, y_ref, tl, nv)

    @pl.when(l == pl.num_programs(1) - 1)
    def _():
        sfin_ref[...] = st_ref[...]


def _scan(w, a, b, k, r, v, s0):
    l, n, gl = w.shape
    nv = v.shape[1]
    tl = _row_tile(l, 64)
    rows = pl.BlockSpec((tl, n, LANES), lambda g, i: (i, 0, g))
    vals = pl.BlockSpec((tl, nv, LANES), lambda g, i: (i, 0, g))
    state = pl.BlockSpec((nv, n, LANES), lambda g, i: (0, 0, g))
    return pl.pallas_call(
        functools.partial(_scan_kernel, tl=tl, nv=nv),
        grid=(gl // LANES, l // tl),
        in_specs=[rows] * 5 + [vals, state],
        out_specs=[vals, state],
        out_shape=[jax.ShapeDtypeStruct((l, nv, gl), F32), jax.ShapeDtypeStruct((nv, n, gl), F32)],
        scratch_shapes=[pltpu.VMEM((nv, n, LANES), F32)],
        compiler_params=_cparams("parallel", "arbitrary"),
        name="rwkv_scan",
    )(w, a, b, k, r, v, s0)


def _keys_to_lanes(x, n_heads):
    pieces = [x[:, h * HEAD:(h + 1) * HEAD] for h in range(n_heads)]
    return jnp.concatenate(pieces + pieces, axis=0).T


def _values_to_lanes(x, n_heads):
    half = HEAD // 2
    lo = [x[:, h * HEAD:h * HEAD + half] for h in range(n_heads)]
    hi = [x[:, h * HEAD + half:(h + 1) * HEAD] for h in range(n_heads)]
    return jnp.concatenate(lo + hi, axis=0).T


def _values_from_lanes(y, n_heads, batch):
    yt = y.T
    hb = n_heads * batch
    full = jnp.concatenate([yt[0:hb], yt[hb:2 * hb]], axis=1)
    return jnp.concatenate([full[h * batch:(h + 1) * batch] for h in range(n_heads)], axis=1)


N_KEY_ARRAYS = 5
N_SLOTS = 8
LOOKAHEAD = 2


def _scan_tokens_kernel(*refs, tl, nv, n_heads, batch):
    key_refs, v_ref = refs[:N_KEY_ARRAYS], refs[N_KEY_ARRAYS]
    y_ref, sfin_ref, st_ref, sa_ref = refs[N_KEY_ARRAYS + 1:N_KEY_ARRAYS + 5]
    slot_refs = refs[N_KEY_ARRAYS + 5:]
    n_slot = N_KEY_ARRAYS + 2
    slots = [slot_refs[n_slot * q:n_slot * (q + 1)] for q in range(N_SLOTS)]
    l = pl.program_id(0)

    @pl.when(l == 0)
    def _():
        st_ref[...] = jnp.zeros_like(st_ref)
        slots[-1][-1][...] = jnp.zeros_like(slots[-1][-1])

    def fill(slot, t):
        for src, dst in zip(key_refs, slot[:N_KEY_ARRAYS]):
            dst[...] = _keys_to_lanes(src[:, t, :], n_heads)
        slot[N_KEY_ARRAYS][...] = _values_to_lanes(v_ref[:, t, :], n_heads)

    for q in range(LOOKAHEAD):
        fill(slots[q], q)

    def one_step(cur, ahead, behind, t_ahead, t_behind):
        wt, at, bt, kt, rt, vt, yt = cur

        def relayout(job):
            if job == 0:
                y_ref[:, t_behind, :] = _values_from_lanes(behind[-1][...], n_heads, batch)
            elif job <= N_KEY_ARRAYS:
                ahead[job - 1][...] = _keys_to_lanes(key_refs[job - 1][:, t_ahead, :], n_heads)
            elif job == N_KEY_ARRAYS + 1:
                ahead[N_KEY_ARRAYS][...] = _values_to_lanes(v_ref[:, t_ahead, :], n_heads)

        for i in range(nv):
            sa_ref[pl.ds(i, 1), :] = jnp.sum(st_ref[i] * at[...], axis=0, keepdims=True)
            if i < n_slot:
                relayout(i)
        for i in range(nv):
            s = st_ref[i] * wt[...] + sa_ref[pl.ds(i, 1), :] * bt[...] + vt[pl.ds(i, 1), :] * kt[...]
            st_ref[i] = s
            yt[pl.ds(i, 1), :] = jnp.sum(s * rt[...], axis=0, keepdims=True)

    def steps(j, carry):
        for q in range(N_SLOTS):
            t = N_SLOTS * j + q
            one_step(slots[q], slots[(q + LOOKAHEAD) % N_SLOTS], slots[(q - 1) % N_SLOTS],
                     jnp.minimum(t + LOOKAHEAD, tl - 1), jnp.maximum(t - 1, 0))
        return carry

    lax.fori_loop(0, tl // N_SLOTS, steps, 0)
    y_ref[:, tl - 1, :] = _values_from_lanes(slots[-1][-1][...], n_heads, batch)

    @pl.when(l == pl.num_programs(0) - 1)
    def _():
        sfin_ref[...] = st_ref[...]


def _scan_tokens(w, a, b, k, r, v):
    batch, l, d = w.shape
    n_heads = d // HEAD
    assert 2 * batch * n_heads == LANES
    nv = HEAD // 2
    tl = _row_tile(l, 64)
    tok = pl.BlockSpec((batch, tl, d), lambda i: (0, i, 0))
    state = pl.BlockSpec((nv, HEAD, LANES), lambda i: (0, 0, 0))
    assert tl % N_SLOTS == 0 and LOOKAHEAD < N_SLOTS - 1
    slot = [pltpu.VMEM((HEAD, LANES), F32)] * N_KEY_ARRAYS + [pltpu.VMEM((nv, LANES), F32)] * 2
    return pl.pallas_call(
        functools.partial(_scan_tokens_kernel, tl=tl, nv=nv, n_heads=n_heads, batch=batch),
        grid=(l // tl,),
        in_specs=[tok] * 6,
        out_specs=[tok, state],
        out_shape=[jax.ShapeDtypeStruct((batch, l, d), F32), jax.ShapeDtypeStruct((nv, HEAD, LANES), F32)],
        scratch_shapes=[pltpu.VMEM((nv, HEAD, LANES), F32), pltpu.VMEM((nv, LANES), F32)] + slot * N_SLOTS,
        compiler_params=_cparams("arbitrary"),
        name="rwkv_scan_tokens",
    )(w, a, b, k, r, v)


def _cmlp_norm(pc, lng, lnb, bd, d_c):
    ge = _gelu(pc)
    u = ge[:, :d_c]
    v = ge[:, d_c:]
    mean = _group_sum(v, bd) * (1.0 / HEAD)
    d = v - mean
    var = _group_sum(d * d, bd) * (1.0 / HEAD)
    return u, d * lax.rsqrt(var + LN_EPS) * lng + lnb


def _cmlp_kernel(pc_ref, lng_ref, lnb_ref, ws_ref, bs_ref, bd_ref, yc_ref, *, d_c, n_chunks):
    u, vn = _cmlp_norm(pc_ref[...], lng_ref[...], lnb_ref[...], bd_ref[...], d_c)
    vb = vn.astype(BF16)
    ri = lax.broadcasted_iota(jnp.int32, (CHUNK, CHUNK), 0)
    ci = lax.broadcasted_iota(jnp.int32, (CHUNK, CHUNK), 1)
    lane = lax.broadcasted_iota(jnp.int32, (CHUNK, LANES), 1)
    n_groups = d_c // HEAD
    wsm = [jnp.where(ri >= ci, ws_ref[g], 0.0).astype(BF16) for g in range(n_groups)]
    for c in range(n_chunks):
        rows = slice(c * CHUNK, (c + 1) * CHUNK)
        for p in range(n_groups // 2):
            cols = slice(p * LANES, (p + 1) * LANES)
            vp = vb[rows, cols]
            s0 = jnp.dot(wsm[2 * p], vp, preferred_element_type=F32)
            s1 = jnp.dot(wsm[2 * p + 1], vp, preferred_element_type=F32)
            s = jnp.where(lane < HEAD, s0, s1) + bs_ref[:, cols]
            yc_ref[rows, cols] = u[rows, cols] * s


def _cmlp(pc, lng, lnb, ws, bs_rows, bd):
    t, p_c = pc.shape
    d_c = p_c // 2
    tm = _row_tile(t, 256)
    assert tm % CHUNK == 0
    return pl.pallas_call(
        functools.partial(_cmlp_kernel, d_c=d_c, n_chunks=tm // CHUNK),
        grid=(t // tm,),
        in_specs=[pl.BlockSpec((tm, p_c), lambda i: (i, 0)), _full((1, d_c)), _full((1, d_c)),
                  _full(ws.shape), _full(bs_rows.shape), _full(bd.shape)],
        out_specs=pl.BlockSpec((tm, d_c), lambda i: (i, 0)),
        out_shape=jax.ShapeDtypeStruct((t, d_c), F32),
        compiler_params=_cparams("parallel"),
        name="cmlp",
    )(pc, lng, lnb, ws, bs_rows, bd)


def _cmlp_first_kernel(pc_ref, lng_ref, lnb_ref, w00_ref, b0_ref, bd_ref, yc_ref, vn_ref, *, d_c):
    u, vn = _cmlp_norm(pc_ref[...], lng_ref[...], lnb_ref[...], bd_ref[...], d_c)
    yc_ref[...] = u * (w00_ref[...] * vn + b0_ref[...])
    vn_ref[...] = vn


def _cmlp_first(pc, lng, lnb, w00, b0, bd):
    t, p_c = pc.shape
    d_c = p_c // 2
    return pl.pallas_call(
        functools.partial(_cmlp_first_kernel, d_c=d_c),
        grid=(1,),
        in_specs=[_full(pc.shape), _full((1, d_c)), _full((1, d_c)), _full((1, d_c)), _full((1, d_c)),
                  _full(bd.shape)],
        out_specs=[_full((t, d_c))] * 2,
        out_shape=[jax.ShapeDtypeStruct((t, d_c), F32)] * 2,
        compiler_params=_cparams("arbitrary"),
        name="cmlp_first",
    )(pc, lng, lnb, w00, b0, bd)


def _mix_kernel(x_ref, y_ref, bonus_ref, g_ref, yc_ref, lng_ref, lnb_ref, bd_ref, wo_ref, o_ref, *, d_a):
    bd = bd_ref[...]
    y = y_ref[...]
    mean = _group_sum(y, bd) * (1.0 / HEAD)
    d = y - mean
    var = _group_sum(d * d, bd) * (1.0 / HEAD)
    yn = d * lax.rsqrt(var + GN_EPS) * lng_ref[...] + lnb_ref[...]
    ya = (yn + bonus_ref[...]) * g_ref[...]
    o_ref[...] = (x_ref[...]
                  + jnp.dot(ya.astype(BF16), wo_ref[0:d_a, :], preferred_element_type=F32)
                  + jnp.dot(yc_ref[...].astype(BF16), wo_ref[d_a:, :], preferred_element_type=F32))


def _mix(x, y, bonus, g, yc, lng, lnb, bd, wo_bf):
    t, d = x.shape
    d_a = y.shape[1]
    d_c = yc.shape[1]
    tm = _row_tile(t, 256)
    row = lambda n: pl.BlockSpec((tm, n), lambda i: (i, 0))
    return pl.pallas_call(
        functools.partial(_mix_kernel, d_a=d_a),
        grid=(t // tm,),
        in_specs=[row(d), row(d_a), row(d_a), row(d_a), row(d_c), _full((1, d_a)), _full((1, d_a)),
                  _full(bd.shape), _full(wo_bf.shape)],
        out_specs=row(d),
        out_shape=jax.ShapeDtypeStruct((t, d), F32),
        compiler_params=_cparams("parallel"),
        name="mix_out",
    )(x, y, bonus, g, yc, lng, lnb, bd, wo_bf)


def _peerq_kernel(x_ref, g_ref, wq_ref, sk_ref, xn_ref, st_ref, *, n_hc):
    hb = _rmsnorm(x_ref[...], g_ref[...]).astype(BF16)
    xn_ref[...] = hb
    qb = jnp.dot(hb, wq_ref[...], preferred_element_type=F32).astype(BF16)
    for hc in range(n_hc):
        st_ref[hc] = lax.dot_general(sk_ref[hc % 2], qb[:, hc * LANES:(hc + 1) * LANES],
                                     (((1,), (1,)), ((), ())), preferred_element_type=F32)


def _peerq(x, g, wq_bf, sk_bf):
    t, d = x.shape
    n_hc = wq_bf.shape[1] // LANES
    tm = _row_tile(t, 512)
    return pl.pallas_call(
        functools.partial(_peerq_kernel, n_hc=n_hc),
        grid=(t // tm,),
        in_specs=[pl.BlockSpec((tm, d), lambda i: (i, 0)), _full((1, d)), _full(wq_bf.shape),
                  _full(sk_bf.shape)],
        out_specs=[pl.BlockSpec((tm, d), lambda i: (i, 0)),
                   pl.BlockSpec((n_hc, N_KEYS, tm), lambda i: (0, 0, i))],
        out_shape=[jax.ShapeDtypeStruct((t, d), BF16), jax.ShapeDtypeStruct((n_hc, N_KEYS, t), F32)],
        compiler_params=_cparams("parallel"),
        name="peer_query",
    )(x, g, wq_bf, sk_bf)


def _batcher_network(n):
    pairs = []

    def merge(lo, m, r):
        step = 2 * r
        if step < m:
            merge(lo, m, step)
            merge(lo + r, m, step)
            pairs.extend((i, i + r) for i in range(lo + r, lo + m - r, step))
        else:
            pairs.append((lo, lo + r))

    def sort(lo, m):
        if m > 1:
            sort(lo, m // 2)
            sort(lo + m // 2, m // 2)
            merge(lo, m, 1)

    sort(0, n)
    return pairs


SUBLANES = 8
KEY_SLABS = N_KEYS // SUBLANES
SLAB_NETWORK = _batcher_network(KEY_SLABS)


def _pop_heads(lists, extra_heads, n):
    lists, extra_heads = list(lists), list(extra_heads)
    tops = []
    for r in range(n):
        head = lists[0]
        for e in extra_heads:
            head = jnp.maximum(head, e)
        m = jnp.max(head, axis=0, keepdims=True)
        tops.append(m)
        hit = lists[0] == m
        for d in range(min(n - 1 - r, len(lists))):
            nxt = lists[d + 1] if d + 1 < len(lists) else NEG_INF
            lists[d] = jnp.where(hit, nxt, lists[d])
        if r < n - 1:
            extra_heads = [jnp.where(e == m, NEG_INF, e) for e in extra_heads]
    return tops


def _top_sorted(s, n):
    slabs = [s[SUBLANES * r:SUBLANES * (r + 1)] for r in range(KEY_SLABS)]
    for i, j in SLAB_NETWORK:
        slabs[i], slabs[j] = jnp.maximum(slabs[i], slabs[j]), jnp.minimum(slabs[i], slabs[j])
    return _pop_heads(slabs, [], n)


def _peer_topk_kernel(st_ref, p1_ref, p2_ref, th_ref, *, n_heads):
    k = PEER_TOPK
    n = k + 1
    n_pad = -(-n // SUBLANES) * SUBLANES

    def head(h, carry):
        s1 = st_ref[2 * h]
        s2 = st_ref[2 * h + 1]
        t1 = _top_sorted(s1, n)
        t2 = _top_sorted(s2, n)
        row = lax.broadcasted_iota(jnp.int32, (n_pad, s1.shape[1]), 0)
        v1 = jnp.full((n_pad, s1.shape[1]), NEG_INF, F32)
        for i in range(n):
            v1 = jnp.where(row == i, t1[i], v1)
        row8 = row[0:SUBLANES]
        lists = [v1[0:SUBLANES] + t2[0]]
        for j in range(1, n):
            lists.append(jnp.where(row8 < n // (j + 1), v1[0:SUBLANES] + t2[j], NEG_INF))
        singles = [v1[SUBLANES * b:SUBLANES * (b + 1)] + t2[0] for b in range(1, n_pad // SUBLANES)]
        c = _pop_heads(lists, singles, n)
        m = c[0]
        z = jnp.ones_like(m)
        for r in range(1, k):
            z = z + jnp.exp(c[r] - m)
        inv_z = 1.0 / z
        th_ref[pl.ds(h, 1), :] = jnp.exp(0.5 * (c[k - 1] + c[k]) - m) * inv_z
        p1_ref[h] = jnp.exp(s1 - t1[0]) * inv_z
        p2_ref[h] = jnp.exp(s2 - t2[0])
        return carry

    lax.fori_loop(0, n_heads, head, 0, unroll=4)


def _peer_topk(st):
    n_hc, n, t = st.shape
    n_heads = n_hc // 2
    tk = LANES
    fac = pl.BlockSpec((n_heads, n, tk), lambda i: (0, 0, i))
    return pl.pallas_call(
        functools.partial(_peer_topk_kernel, n_heads=n_heads),
        grid=(t // tk,),
        in_specs=[pl.BlockSpec((n_hc, n, tk), lambda i: (0, 0, i))],
        out_specs=[fac, fac, pl.BlockSpec((n_heads, tk), lambda i: (0, i))],
        out_shape=[jax.ShapeDtypeStruct((n_heads, n, t), F32)] * 2 + [jax.ShapeDtypeStruct((n_heads, t), F32)],
        compiler_params=_cparams("parallel"),
        name="peer_topk",
    )(st)


ROW_BLK = 32
KEYS_PER_TILE = 8


def _peer_dense_kernel(xn_ref, x_ref, u_ref, v_ref, p1_ref, p2_ref, th_ref, o_ref, ht_ref, at_ref, acc_ref,
                       *, n_heads, tm):
    j = pl.program_id(1)

    @pl.when(j == 0)
    def _():
        acc_ref[...] = jnp.zeros_like(acc_ref)

    ht_ref[...] = lax.dot_general(u_ref[...], xn_ref[...], (((1,), (1,)), ((), ())),
                                  preferred_element_type=F32)

    for k in range(KEYS_PER_TILE):
        for lc in range(tm // LANES):
            cols = pl.ds(lc * LANES, LANES)
            p1_rows = [p1_ref[h, k:k + 1, cols] for h in range(n_heads)]
            th_rows = [th_ref[h:h + 1, cols] for h in range(n_heads)]
            for rb in range(N_KEYS // ROW_BLK):
                r2 = pl.ds(rb * ROW_BLK, ROW_BLK)
                r0 = pl.ds(k * N_KEYS + rb * ROW_BLK, ROW_BLK)
                gsum = jnp.zeros((ROW_BLK, LANES), F32)
                for h in range(n_heads):
                    w = p1_rows[h] * p2_ref[h, r2, cols]
                    gsum = gsum + jnp.where(w >= th_rows[h], w, 0.0)
                at_ref[r0, cols] = (_gelu(ht_ref[r0, cols]) * gsum).astype(BF16)

    acc_ref[...] += lax.dot_general(at_ref[...], v_ref[...], (((0,), (0,)), ((), ())),
                                    preferred_element_type=F32)

    @pl.when(j == pl.num_programs(1) - 1)
    def _():
        o_ref[...] = x_ref[...] + acc_ref[...]


def _peer_dense(xn, x, u_bf, v_bf, p1, p2, th):
    t, d = x.shape
    n_exp = u_bf.shape[0]
    n_heads = p1.shape[0]
    tm = _row_tile(t, 512)
    te = KEYS_PER_TILE * N_KEYS
    assert n_exp == N_KEYS * N_KEYS
    return pl.pallas_call(
        functools.partial(_peer_dense_kernel, n_heads=n_heads, tm=tm),
        grid=(t // tm, n_exp // te),
        in_specs=[pl.BlockSpec((tm, d), lambda i, j: (i, 0)),
                  pl.BlockSpec((tm, d), lambda i, j: (i, 0)),
                  pl.BlockSpec((te, d), lambda i, j: (j, 0)),
                  pl.BlockSpec((te, d), lambda i, j: (j, 0)),
                  pl.BlockSpec((n_heads, KEYS_PER_TILE, tm), lambda i, j: (0, j, i)),
                  pl.BlockSpec((n_heads, N_KEYS, tm), lambda i, j: (0, 0, i)),
                  pl.BlockSpec((n_heads, tm), lambda i, j: (0, i))],
        out_specs=pl.BlockSpec((tm, d), lambda i, j: (i, 0)),
        out_shape=jax.ShapeDtypeStruct((t, d), F32),
        scratch_shapes=[pltpu.VMEM((te, tm), F32), pltpu.VMEM((te, tm), BF16), pltpu.VMEM((tm, d), F32)],
        compiler_params=_cparams("parallel", "arbitrary"),
        name="peer_dense",
    )(xn, x, u_bf, v_bf, p1, p2, th)


def _ple_kernel(x_ref, p_ref, g_ref, gw_ref, pw_ref, gf_ref, o_ref, *, final):
    x = x_ref[...]
    hn = _rmsnorm(x, g_ref[...])
    gate = jax.nn.sigmoid(jnp.dot(hn.astype(BF16), gw_ref[...], preferred_element_type=F32))
    e = jnp.dot(p_ref[...].astype(BF16), pw_ref[...], preferred_element_type=F32)
    x = x + e * gate
    o_ref[...] = _rmsnorm(x, gf_ref[...]) if final else x


def _ple(x, p, g, gw_bf, pw_bf, gf, final):
    t, d = x.shape
    dp = p.shape[1]
    tm = _row_tile(t, 512)
    return pl.pallas_call(
        functools.partial(_ple_kernel, final=final),
        grid=(t // tm,),
        in_specs=[pl.BlockSpec((tm, d), lambda i: (i, 0)), pl.BlockSpec((tm, dp), lambda i: (i, 0)),
                  _full((1, d)), _full(gw_bf.shape), _full(pw_bf.shape), _full((1, d))],
        out_specs=pl.BlockSpec((tm, d), lambda i: (i, 0)),
        out_shape=jax.ShapeDtypeStruct((t, d), F32),
        compiler_params=_cparams("parallel"),
        name="ple",
    )(x, p, g, gw_bf, pw_bf, gf)


def _rwkv_prompt_scan(parts, batch, seq, n_heads):
    r, k, v, w, a, b = (x.reshape(batch, seq, n_heads * HEAD) for x in parts)
    y, s_fin = _scan_tokens(w, a, b, k, r, v)
    s_fin = s_fin.reshape(HEAD // 2, HEAD, 2, n_heads, batch).transpose(4, 3, 2, 0, 1)
    return y.reshape(batch * seq, n_heads * HEAD), s_fin.reshape(batch, n_heads, HEAD, HEAD)


def _rwkv_sample_scan(parts, wkv_prev, batch, n_heads):
    bh = batch * n_heads
    assert bh % LANES == 0
    rows = lambda x: x.reshape(batch, n_heads, HEAD).transpose(2, 0, 1).reshape(1, HEAD, bh)
    r, k, v, w, a, b = parts
    s0 = wkv_prev.transpose(2, 3, 0, 1).reshape(HEAD, HEAD, bh)
    y, s_fin = _scan(rows(w), rows(a), rows(b), rows(k), rows(r), rows(v), s0)
    y = y.reshape(HEAD, batch, n_heads).transpose(1, 2, 0).reshape(batch, n_heads * HEAD)
    return y, s_fin.reshape(HEAD, HEAD, batch, n_heads).transpose(2, 3, 0, 1)


def _layer(x, p, wkv_prev, shift_prev, lw, gf, final):
    batch, seq, d = x.shape
    t = batch * seq
    d_a = lw["w0"].shape[1]
    n_heads = d_a // HEAD
    p_a = lw["mu"].shape[1]
    xt = x.reshape(t, d)

    pa, pc = _proj(xt, lw["norm_mix_g"], lw["w_in"], p_a)
    pa3 = pa.reshape(batch, seq, p_a)
    r, k, v, w, a, b, g, bonus = _prep(pa, shift_prev, seq, lw["mu"], lw["w0"], lw["a0"], lw["wa2"], lw["g2"],
                                       lw["k_k"], lw["k_a"], lw["r_k"], lw["bd"])
    if wkv_prev is None:
        y, wkv_new = _rwkv_prompt_scan((r, k, v, w, a, b), batch, seq, n_heads)
        yc = _cmlp(pc, lw["cmlp_ln_g"], lw["cmlp_ln_b"], lw["cmlp_ws"], lw["cmlp_bs_rows"], lw["bd"])
        v_rows = None
    else:
        assert seq == 1
        y, wkv_new = _rwkv_sample_scan((r, k, v, w, a, b), wkv_prev, batch, n_heads)
        yc, v_rows = _cmlp_first(pc, lw["cmlp_ln_g"], lw["cmlp_ln_b"], lw["cmlp_w00"], lw["cmlp_b0"], lw["bd"])
        v_rows = v_rows.reshape(batch, seq, -1)
    x1 = _mix(xt, y, bonus, g, yc, lw["rwkv_ln_g"], lw["rwkv_ln_b"], lw["bd"], lw["w_out"])

    xn, st = _peerq(x1, lw["norm_ffn_g"], lw["peer_wq"], lw["peer_subkeys"])
    p1, p2, th = _peer_topk(st)
    x2 = _peer_dense(xn, x1, lw["peer_u"], lw["peer_v"], p1, p2, th)

    x3 = _ple(x2, p.reshape(t, -1), lw["norm_ple_g"], lw["ple_gate_w"], lw["ple_w"], gf, final)
    return x3.reshape(batch, seq, d), wkv_new, pa3[:, -1], v_rows


def kernel(x_prompt, x_sample, state_wkv, state_shift, p_prompt, p_sample, norm_mix_g, w_in, shift_mu, rwkv_w0, rwkv_w2, rwkv_a0, rwkv_a2, rwkv_g2, rwkv_k_k, rwkv_k_a, rwkv_r_k, rwkv_ln_g, rwkv_ln_b, cmlp_ln_g, cmlp_ln_b, cmlp_ws, cmlp_bs, w_out, norm_ffn_g, peer_wq, peer_subkeys, peer_u, peer_v, norm_ple_g, ple_w, ple_gate_w, norm_final_g):
    depth = state_wkv.shape[0]
    batch = x_prompt.shape[0]
    d_a = rwkv_w0.shape[1]
    d_c = cmlp_ln_g.shape[1]
    p_a = shift_mu.shape[1]
    lora_w = rwkv_w2.shape[1]
    lora_a = rwkv_a2.shape[1]
    assert lora_w == HEAD and lora_a == HEAD and d_a == d_c
    gidx = jnp.arange(d_a) // HEAD
    bd = (gidx[:, None] == gidx[None, :]).astype(BF16)
    row = lambda z: z.reshape(1, -1)
    gf = row(norm_final_g)

    hp, hs = x_prompt, x_sample
    outs = [[] for _ in range(5)]
    for i in range(depth):
        wa2 = jnp.zeros((lora_w + lora_a, 2 * d_a), F32)
        wa2 = wa2.at[:lora_w, :d_a].set(rwkv_w2[i]).at[lora_w:, d_a:].set(rwkv_a2[i])
        lw = dict(
            norm_mix_g=row(norm_mix_g[i]), w_in=w_in[i].astype(BF16), mu=row(shift_mu[i]),
            w0=row(rwkv_w0[i]), a0=row(rwkv_a0[i]), wa2=wa2.astype(BF16), g2=rwkv_g2[i].astype(BF16),
            k_k=row(rwkv_k_k[i]), k_a=row(rwkv_k_a[i]), r_k=row(rwkv_r_k[i]), bd=bd,
            rwkv_ln_g=row(rwkv_ln_g[i]), rwkv_ln_b=row(rwkv_ln_b[i]),
            cmlp_ln_g=row(cmlp_ln_g[i]), cmlp_ln_b=row(cmlp_ln_b[i]), cmlp_ws=cmlp_ws[i],
            cmlp_bs_rows=jnp.repeat(cmlp_bs[i].T, HEAD, axis=1),
            cmlp_w00=row(jnp.repeat(cmlp_ws[i][:, 0, 0], HEAD)), cmlp_b0=row(jnp.repeat(cmlp_bs[i][:, 0], HEAD)),
            w_out=w_out[i].astype(BF16), norm_ffn_g=row(norm_ffn_g[i]), peer_wq=peer_wq[i].astype(BF16),
            peer_subkeys=peer_subkeys[i].astype(BF16), peer_u=peer_u[i].astype(BF16),
            peer_v=peer_v[i].astype(BF16), norm_ple_g=row(norm_ple_g[i]),
            ple_w=ple_w[i].astype(BF16), ple_gate_w=ple_gate_w[i].astype(BF16),
        )
        final = i == depth - 1
        shift0 = jnp.zeros((batch, p_a), F32)
        hp, wkv_p, shift_p, _ = _layer(hp, p_prompt[i], None, shift0, lw, gf, final)
        hs, wkv_s, shift_s, v_s = _layer(hs, p_sample[i], state_wkv[i], state_shift[i], lw, gf, final)
        for o, val in zip(outs, (wkv_p, shift_p, wkv_s, shift_s, v_s)):
            o.append(val)
    return (hp, hs) + tuple(jnp.stack(o) for o in outs)
```
